```python
import math
import jax
import jax.numpy as jnp
from jax import lax
import numpy as np

D_MODEL = 4096
BATCH = 1
SEQ = 8192
DEPTH = 1
DEC_BATCH = 32
DEC_SEQ = 4
PAST_LEN = 8192
PAGE_SIZE = 128

N_META = 16
HEAD_DIM = 128
MIX_WIDTH = D_MODEL
GDN_WIDTH = MIX_WIDTH // 2
SB_WIDTH = MIX_WIDTH - GDN_WIDTH
GDN_HEADS = GDN_WIDTH // HEAD_DIM
SB_HEADS = SB_WIDTH // HEAD_DIM
CONV_WIDTH = 4
CONV_CH = 3 * GDN_WIDTH
GDN_CHUNK = 64
SB_BLOCK = 128
SB_BIAS_MIN = -9.0
SB_BIAS_MAX = -3.0
D_FF = ((8 * D_MODEL // 3 + 255) // 256) * 256
FFN_RESIDUAL_WEIGHT = 0.5
NORM_EPS = 1e-6
L2_EPS = 1e-6
GDN_Q_SCALE = HEAD_DIM ** -0.5
SB_SCALE = HEAD_DIM ** -0.5
DT_MIN = 1e-3
DT_MAX = 1e-1
PROJ_SPLITS = (CONV_CH, GDN_WIDTH, GDN_HEADS, GDN_HEADS, SB_WIDTH, SB_WIDTH, SB_WIDTH)
PROJ_COLS = CONV_CH + GDN_WIDTH + 2 * GDN_HEADS + 3 * SB_WIDTH

kernel_name = 'hymba_gdn_stickbreak_decode_step'


def rms_norm(x, w):
    xf = x.astype(jnp.float32)
    y = xf * lax.rsqrt(jnp.mean(xf * xf, axis=-1, keepdims=True) + NORM_EPS)
    return (y * w.astype(jnp.float32)).astype(x.dtype)


def half_step_ffn(h, norm_w, w_gate, w_up, w_down):
    u = rms_norm(h, norm_w)
    return h + FFN_RESIDUAL_WEIGHT * ((jax.nn.silu(u @ w_gate) * (u @ w_up)) @ w_down)


def head_rms(o, w):
    return o * lax.rsqrt(jnp.mean(o * o, axis=-1, keepdims=True) + NORM_EPS) * w.astype(jnp.float32)


def l2_normalize(t):
    return t * lax.rsqrt(jnp.sum(t * t, axis=-1, keepdims=True) + L2_EPS)


def split_projection(u, w_in):
    bounds = [int(b) for b in np.cumsum(PROJ_SPLITS)[:-1]]
    return jnp.split(u @ w_in, bounds, axis=-1)


def short_conv(prev, x, conv_w):
    xc = jnp.concatenate([prev.astype(x.dtype), x], axis=1)
    y = lax.conv_general_dilated(xc, conv_w[:, None, :].astype(x.dtype), window_strides=(1,), padding='VALID',
                                 dimension_numbers=('NWC', 'WIO', 'NWC'), feature_group_count=CONV_CH)
    return y, xc[:, xc.shape[1] - (CONV_WIDTH - 1):]


def gdn_features(qkv, b, a, a_log, dt_bias):
    bsz, t, _ = qkv.shape
    qkv = jax.nn.silu(qkv.astype(jnp.float32)).reshape(bsz, t, 3, GDN_HEADS, HEAD_DIM).transpose(2, 0, 3, 1, 4)
    q = l2_normalize(qkv[0]) * GDN_Q_SCALE
    k = l2_normalize(qkv[1])
    v = qkv[2]
    beta = jax.nn.sigmoid(b.astype(jnp.float32)).transpose(0, 2, 1)
    g = (-jnp.exp(a_log.astype(jnp.float32)) * jax.nn.softplus(a.astype(jnp.float32) + dt_bias.astype(jnp.float32))).transpose(0, 2, 1)
    return q, k, v, g, beta


def gdn_chunk(s, q, k, v, g, beta):
    c = q.shape[2]
    idx = jnp.arange(c)
    causal = idx[:, None] >= idx[None, :]
    strict = idx[:, None] > idx[None, :]
    gc = jnp.cumsum(g, axis=-1)
    decay = jnp.exp(jnp.where(causal, gc[..., :, None] - gc[..., None, :], -jnp.inf))
    a_kk = jnp.where(strict, beta[..., :, None] * jnp.einsum('bhik,bhjk->bhij', k, k) * decay, 0.0)
    lhs = a_kk + jnp.eye(c, dtype=jnp.float32)
    rhs = jnp.concatenate([beta[..., None] * v, (beta * jnp.exp(gc))[..., None] * k], axis=-1)
    sol = lax.linalg.triangular_solve(lhs, rhs, left_side=True, lower=True, unit_diagonal=True)
    u, w = sol[..., :HEAD_DIM], sol[..., HEAD_DIM:]
    v_new = u - jnp.einsum('bhck,bhkv->bhcv', w, s)
    qk = jnp.einsum('bhik,bhjk->bhij', q, k) * decay
    o = jnp.einsum('bhck,bhkv->bhcv', q * jnp.exp(gc)[..., None], s) + jnp.einsum('bhij,bhjv->bhiv', qk, v_new)
    g_end = gc[..., -1:]
    s_new = jnp.exp(g_end)[..., None] * s + jnp.einsum('bhck,bhcv->bhkv', k * jnp.exp(g_end - gc)[..., None], v_new)
    return s_new, o


def gdn_output(o, z, norm_w):
    bsz, _, t, _ = o.shape
    o = head_rms(o.transpose(0, 2, 1, 3), norm_w)
    gate = jax.nn.silu(z.astype(jnp.float32)).reshape(bsz, t, GDN_HEADS, HEAD_DIM)
    return (o * gate).reshape(bsz, t, GDN_WIDTH)


def sb_weights(z, q_pos, k_pos):
    mask = k_pos[None, :] < q_pos[:, None]
    sp = jnp.where(mask, jax.nn.softplus(z), 0.0)
    c = lax.cumsum(sp, axis=3, reverse=True)
    return jnp.exp(jnp.where(mask, z - c, -jnp.inf))


def sb_read(q, k_parts, v_parts, q_pos, k_pos, sb_bias):
    z = jnp.concatenate([jnp.einsum('bthd,bshd->bhts', q, kp, preferred_element_type=jnp.float32) for kp in k_parts], axis=3) * SB_SCALE
    z = z + sb_bias.astype(jnp.float32)[None, :, None, None]
    w = sb_weights(z, q_pos, k_pos)
    out = None
    start = 0
    for vp in v_parts:
        n = vp.shape[1]
        part = jnp.einsum('bhts,bshd->bthd', w[..., start:start + n].astype(vp.dtype), vp, preferred_element_type=jnp.float32)
        out = part if out is None else out + part
        start += n
    return out


def sb_output(o, norm_w):
    bsz, t, _, _ = o.shape
    return head_rms(o, norm_w).reshape(bsz, t, SB_WIDTH)


def merge_heads(o_gdn, o_sb, w_out, dtype):
    return jnp.concatenate([o_gdn, o_sb], axis=-1).astype(dtype) @ w_out


def prompt_mixer(u, w_in, conv_w, a_log, dt_bias, gdn_norm_w, sb_norm_w, sb_bias, w_out):
    bsz, length, _ = u.shape
    n_real = length - N_META
    qkv, z, b, a, q_sb, k_sb, v_sb = split_projection(u, w_in)
    qkv_c, conv_state = short_conv(jnp.zeros((bsz, CONV_WIDTH - 1, CONV_CH), u.dtype), qkv, conv_w)
    q, k, v, g, beta = gdn_features(qkv_c, b, a, a_log, dt_bias)
    s0 = jnp.zeros((bsz, GDN_HEADS, HEAD_DIM, HEAD_DIM), jnp.float32)
    s_meta, o_meta = gdn_chunk(s0, q[:, :, :N_META], k[:, :, :N_META], v[:, :, :N_META], g[:, :, :N_META], beta[:, :, :N_META])

    def to_chunks(t):
        t = t[:, :, N_META:]
        t = t.reshape(t.shape[:2] + (n_real // GDN_CHUNK, GDN_CHUNK) + t.shape[3:])
        return jnp.moveaxis(t, 2, 0)

    s_fin, o_chunks = lax.scan(lambda s, xs: gdn_chunk(s, *xs), s_meta,
                               (to_chunks(q), to_chunks(k), to_chunks(v), to_chunks(g), to_chunks(beta)))
    o_real = jnp.moveaxis(o_chunks, 0, 2).reshape(bsz, GDN_HEADS, n_real, HEAD_DIM)
    o_gdn = gdn_output(jnp.concatenate([o_meta, o_real], axis=2), z, gdn_norm_w)
    qh = q_sb.reshape(bsz, length, SB_HEADS, HEAD_DIM)
    kh = k_sb.reshape(bsz, length, SB_HEADS, HEAD_DIM)
    vh = v_sb.reshape(bsz, length, SB_HEADS, HEAD_DIM)
    pos = jnp.arange(length)
    o_sb_meta = sb_read(qh[:, :N_META], (kh[:, :N_META],), (vh[:, :N_META],), pos[:N_META], pos[:N_META], sb_bias)

    def query_block(start):
        q_blk = lax.dynamic_slice_in_dim(qh, start, SB_BLOCK, axis=1)
        return sb_read(q_blk, (kh,), (vh,), start + jnp.arange(SB_BLOCK), pos, sb_bias)

    starts = N_META + SB_BLOCK * jnp.arange(n_real // SB_BLOCK)
    o_blocks = lax.map(query_block, starts)
    o_sb_real = jnp.moveaxis(o_blocks, 0, 1).reshape(bsz, n_real, SB_HEADS, HEAD_DIM)
    o_sb = sb_output(jnp.concatenate([o_sb_meta, o_sb_real], axis=1), sb_norm_w)
    out = merge_heads(o_gdn, o_sb, w_out, u.dtype)
    return out, kh, vh, s_fin, conv_state


def sample_mixer(u, cache_k, cache_v, page_table, layer, state_s, state_conv,
                 w_in, conv_w, a_log, dt_bias, gdn_norm_w, sb_norm_w, sb_bias, w_out):
    bsz, t, _ = u.shape
    qkv, z, b, a, q_sb, k_sb, v_sb = split_projection(u, w_in)
    qkv_c, conv_state = short_conv(state_conv, qkv, conv_w)
    q, k, v, g, beta = gdn_features(qkv_c, b, a, a_log, dt_bias)
    s_new, o = gdn_chunk(state_s.astype(jnp.float32), q, k, v, g, beta)
    o_gdn = gdn_output(o, z, gdn_norm_w)
    k_rows = k_sb.reshape(bsz, t, SB_HEADS, HEAD_DIM)
    v_rows = v_sb.reshape(bsz, t, SB_HEADS, HEAD_DIM)
    k_past = cache_k[layer][page_table]
    v_past = cache_v[layer][page_table]
    past_len = k_past.shape[1] * k_past.shape[2]
    k_past = k_past.reshape(bsz, past_len, SB_HEADS, HEAD_DIM)
    v_past = v_past.reshape(bsz, past_len, SB_HEADS, HEAD_DIM)
    o_sb = sb_read(q_sb.reshape(bsz, t, SB_HEADS, HEAD_DIM), (k_past, k_rows), (v_past, v_rows),
                   past_len + jnp.arange(t), jnp.arange(past_len + t), sb_bias)
    out = merge_heads(o_gdn, sb_output(o_sb, sb_norm_w), w_out, u.dtype)
    return out, k_rows, v_rows, s_new, conv_state


def setup_inputs(seed: int = 0) -> dict:
    key = jax.random.key(seed)
    ks = jax.random.split(key, 26)
    n_pages = PAST_LEN // PAGE_SIZE
    n_used = DEC_BATCH * n_pages
    n_pool = n_used + (n_used + 3) // 4

    def normal(k, shape, scale):
        return scale * jax.random.normal(k, shape, jnp.float32)

    def gain(k, shape):
        return 1.0 + 0.02 * jax.random.normal(k, shape, jnp.float32)

    page_table = jax.random.permutation(ks[4], n_pool)[:n_used].reshape(DEC_BATCH, n_pages).astype(jnp.int32)
    dt = jnp.exp(jax.random.uniform(ks[14], (DEPTH, GDN_HEADS), jnp.float32, math.log(DT_MIN), math.log(DT_MAX)))
    dt_bias = dt + jnp.log(-jnp.expm1(-dt))
    a_log = jnp.log(jax.random.uniform(ks[13], (DEPTH, GDN_HEADS), jnp.float32, 1.0, 16.0))
    sb_bias = jax.random.uniform(ks[24], (DEPTH, SB_HEADS), jnp.float32, SB_BIAS_MIN, SB_BIAS_MAX)
    return {
        'x_prompt': normal(ks[0], (BATCH, SEQ, D_MODEL), 1.0),
        'x_sample': normal(ks[1], (DEC_BATCH, DEC_SEQ, D_MODEL), 1.0),
        'cache_sb_k': normal(ks[2], (DEPTH, n_pool, PAGE_SIZE, SB_HEADS, HEAD_DIM), 1.0),
        'cache_sb_v': normal(ks[3], (DEPTH, n_pool, PAGE_SIZE, SB_HEADS, HEAD_DIM), 1.0),
        'page_table': page_table,
        'state_gdn': normal(ks[5], (DEPTH, DEC_BATCH, GDN_HEADS, HEAD_DIM, HEAD_DIM), 0.1),
        'state_gdn_conv': normal(ks[6], (DEPTH, DEC_BATCH, CONV_WIDTH - 1, CONV_CH), 1.0),
        'meta_tokens': normal(ks[7], (N_META, D_MODEL), 1.0),
        'norm_ffn1': gain(ks[8], (DEPTH, D_MODEL)),
        'ffn1_w_gate': normal(ks[9], (DEPTH, D_MODEL, D_FF), D_MODEL ** -0.5),
        'ffn1_w_up': normal(ks[10], (DEPTH, D_MODEL, D_FF), D_MODEL ** -0.5),
        'ffn1_w_down': normal(ks[11], (DEPTH, D_FF, D_MODEL), D_FF ** -0.5),
        'norm_mix': gain(ks[12], (DEPTH, D_MODEL)),
        'w_in': normal(ks[15], (DEPTH, D_MODEL, PROJ_COLS), D_MODEL ** -0.5),
        'gdn_conv_w': normal(ks[16], (DEPTH, CONV_WIDTH, CONV_CH), CONV_WIDTH ** -0.5),
        'gdn_a_log': a_log,
        'gdn_dt_bias': dt_bias,
        'gdn_norm_w': gain(ks[17], (DEPTH, HEAD_DIM)),
        'sb_norm_w': gain(ks[18], (DEPTH, HEAD_DIM)),
        'sb_bias': sb_bias,
        'w_out': normal(ks[19], (DEPTH, MIX_WIDTH, D_MODEL), MIX_WIDTH ** -0.5),
        'norm_ffn2': gain(ks[20], (DEPTH, D_MODEL)),
        'ffn2_w_gate': normal(ks[21], (DEPTH, D_MODEL, D_FF), D_MODEL ** -0.5),
        'ffn2_w_up': normal(ks[22], (DEPTH, D_MODEL, D_FF), D_MODEL ** -0.5),
        'ffn2_w_down': normal(ks[23], (DEPTH, D_FF, D_MODEL), D_FF ** -0.5),
        'norm_final': gain(ks[25], (D_MODEL,)),
    }


def reference(x_prompt, x_sample, cache_sb_k, cache_sb_v, page_table, state_gdn, state_gdn_conv,
              meta_tokens, norm_ffn1, ffn1_w_gate, ffn1_w_up, ffn1_w_down, norm_mix, w_in,
              gdn_conv_w, gdn_a_log, gdn_dt_bias, gdn_norm_w, sb_norm_w, sb_bias, w_out,
              norm_ffn2, ffn2_w_gate, ffn2_w_up, ffn2_w_down, norm_final):
    bsz = x_prompt.shape[0]
    meta = jnp.broadcast_to(meta_tokens.astype(x_prompt.dtype)[None], (bsz, N_META, D_MODEL))
    hp = jnp.concatenate([meta, x_prompt], axis=1)
    hs = x_sample
    kp_list, vp_list, ks_list, vs_list = [], [], [], []
    sp_list, ss_list, cp_list, cs_list = [], [], [], []
    for layer in range(DEPTH):
        hp = half_step_ffn(hp, norm_ffn1[layer], ffn1_w_gate[layer], ffn1_w_up[layer], ffn1_w_down[layer])
        hs = half_step_ffn(hs, norm_ffn1[layer], ffn1_w_gate[layer], ffn1_w_up[layer], ffn1_w_down[layer])
        mix_p, k_p, v_p, s_p, c_p = prompt_mixer(
            rms_norm(hp, norm_mix[layer]), w_in[layer], gdn_conv_w[layer], gdn_a_log[layer],
            gdn_dt_bias[layer], gdn_norm_w[layer], sb_norm_w[layer], sb_bias[layer], w_out[layer])
        mix_s, k_s, v_s, s_s, c_s = sample_mixer(
            rms_norm(hs, norm_mix[layer]), cache_sb_k, cache_sb_v, page_table, layer,
            state_gdn[layer], state_gdn_conv[layer], w_in[layer], gdn_conv_w[layer], gdn_a_log[layer],
            gdn_dt_bias[layer], gdn_norm_w[layer], sb_norm_w[layer], sb_bias[layer], w_out[layer])
        hp = hp + mix_p
        hs = hs + mix_s
        hp = half_step_ffn(hp, norm_ffn2[layer], ffn2_w_gate[layer], ffn2_w_up[layer], ffn2_w_down[layer])
        hs = half_step_ffn(hs, norm_ffn2[layer], ffn2_w_gate[layer], ffn2_w_up[layer], ffn2_w_down[layer])
        kp_list.append(k_p.astype(cache_sb_k.dtype))
        vp_list.append(v_p.astype(cache_sb_v.dtype))
        ks_list.append(k_s.astype(cache_sb_k.dtype))
        vs_list.append(v_s.astype(cache_sb_v.dtype))
        sp_list.append(s_p.astype(state_gdn.dtype))
        ss_list.append(s_s.astype(state_gdn.dtype))
        cp_list.append(c_p.astype(state_gdn_conv.dtype))
        cs_list.append(c_s.astype(state_gdn_conv.dtype))
    y_prompt = rms_norm(hp[:, N_META:], norm_final)
    y_sample = rms_norm(hs, norm_final)
    return (y_prompt, y_sample, jnp.stack(kp_list), jnp.stack(vp_list), jnp.stack(ks_list), jnp.stack(vs_list),
            jnp.stack(sp_list), jnp.stack(ss_list), jnp.stack(cp_list), jnp.stack(cs_list))
```

```python
import functools
import math

import jax
import jax.numpy as jnp
from jax import lax
from jax.experimental import pallas as pl
from jax.experimental.pallas import tpu as pltpu

F32 = jnp.float32
BF16 = jnp.bfloat16
HIGHEST = lax.Precision.HIGHEST

HEAD_DIM = 128
LANES = 128
SUBLANES = 8
NORM_EPS = 1e-6
L2_EPS = 1e-6
FFN_RESIDUAL_WEIGHT = 0.5
GDN_CHUNK = 64
GDN_Q_SCALE = HEAD_DIM ** -0.5
SB_SCALE = HEAD_DIM ** -0.5
SB_BLOCK = 256
SAMPLE_ROWS = 8
VMEM_LIMIT_BYTES = 58 * 1024 * 1024


def _round_up(x, m):
    return (x + m - 1) // m * m


def _row_tile(rows, cap):
    best = 16
    for t in range(16, cap + 1, 16):
        if rows % t == 0:
            best = t
    assert rows % best == 0
    return best


def _col_tile(cols, cap):
    best = LANES
    for t in range(LANES, cap + 1, LANES):
        if cols % t == 0:
            best = t
    assert cols % best == 0
    return best


def _softplus(x):
    return jnp.maximum(x, 0.0) + jnp.log1p(jnp.exp(-jnp.abs(x)))


def _silu(x):
    return x * jax.nn.sigmoid(x)


def _rms_rows(x, w):
    return x * lax.rsqrt(jnp.mean(x * x, axis=-1, keepdims=True) + NORM_EPS) * w


def _params(*semantics):
    return pltpu.CompilerParams(dimension_semantics=semantics, vmem_limit_bytes=VMEM_LIMIT_BYTES)


def _ffn_kernel(x_ref, nw_ref, wg_ref, wu_ref, wd_ref, fw_ref, o_ref, u_ref, *, final_norm):
    j = pl.program_id(1)

    @pl.when(j == 0)
    def _():
        x = x_ref[...]
        u_ref[...] = _rms_rows(x, nw_ref[...]).astype(BF16)
        o_ref[...] = x

    u = u_ref[...]
    g = jnp.dot(u, wg_ref[...], preferred_element_type=F32)
    up = jnp.dot(u, wu_ref[...], preferred_element_type=F32)
    a = (_silu(g) * up).astype(BF16)
    o_ref[...] += FFN_RESIDUAL_WEIGHT * jnp.dot(a, wd_ref[...], preferred_element_type=F32)

    if final_norm:
        @pl.when(j == pl.num_programs(1) - 1)
        def _():
            o_ref[...] = _rms_rows(o_ref[...], fw_ref[...])


def _ffn(h, norm_w, wg, wu, wd, final_w, final_norm):
    rows, d = h.shape
    f = wg.shape[1]
    tm = _row_tile(rows, 640)
    tf = _col_tile(f, 256)
    return pl.pallas_call(
        functools.partial(_ffn_kernel, final_norm=final_norm),
        grid=(rows // tm, f // tf),
        in_specs=[
            pl.BlockSpec((tm, d), lambda i, j: (i, 0), pipeline_mode=pl.Buffered(1)),
            pl.BlockSpec((1, d), lambda i, j: (0, 0)),
            pl.BlockSpec((d, tf), lambda i, j: (0, j)),
            pl.BlockSpec((d, tf), lambda i, j: (0, j)),
            pl.BlockSpec((tf, d), lambda i, j: (j, 0)),
            pl.BlockSpec((1, d), lambda i, j: (0, 0)),
        ],
        out_specs=pl.BlockSpec((tm, d), lambda i, j: (i, 0)),
        out_shape=jax.ShapeDtypeStruct((rows, d), F32),
        scratch_shapes=[pltpu.VMEM((tm, d), BF16)],
        compiler_params=_params("parallel", "arbitrary"),
        name="ffn_half_step",
    )(h, norm_w.reshape(1, d), wg, wu, wd, final_w.reshape(1, d))


def _proj_kernel(x_ref, nw_ref, w_ref, o_ref, u_ref):
    @pl.when(pl.program_id(1) == 0)
    def _():
        u_ref[...] = _rms_rows(x_ref[...], nw_ref[...]).astype(BF16)

    o_ref[...] = jnp.dot(u_ref[...], w_ref[...], preferred_element_type=F32)


def _proj(h, norm_w, w):
    rows, d = h.shape
    n = w.shape[1]
    tm = _row_tile(rows, 640)
    tn = _col_tile(n, 768)
    return pl.pallas_call(
        _proj_kernel,
        grid=(rows // tm, n // tn),
        in_specs=[
            pl.BlockSpec((tm, d), lambda i, j: (i, 0)),
            pl.BlockSpec((1, d), lambda i, j: (0, 0)),
            pl.BlockSpec((d, tn), lambda i, j: (0, j)),
        ],
        out_specs=pl.BlockSpec((tm, tn), lambda i, j: (i, j)),
        out_shape=jax.ShapeDtypeStruct((rows, n), F32),
        scratch_shapes=[pltpu.VMEM((tm, d), BF16)],
        compiler_params=_params("parallel", "arbitrary"),
        name="mixer_in_proj",
    )(h, norm_w.reshape(1, d), w)


def _outproj_kernel(og_ref, os_ref, wg_ref, ws_ref, h_ref, o_ref):
    o_ref[...] = (h_ref[...]
                  + jnp.dot(og_ref[...], wg_ref[...], preferred_element_type=F32)
                  + jnp.dot(os_ref[...], ws_ref[...], preferred_element_type=F32))


def _outproj(o_gdn, o_sb, w_out, h):
    rows, d = h.shape
    gw, sw = o_gdn.shape[1], o_sb.shape[1]
    assert gw == sw, "head groups are assumed equally wide"
    tm = _row_tile(rows, 640)
    tn = _col_tile(d, 1024)
    return pl.pallas_call(
        _outproj_kernel,
        grid=(rows // tm, d // tn),
        in_specs=[
            pl.BlockSpec((tm, gw), lambda i, j: (i, 0)),
            pl.BlockSpec((tm, sw), lambda i, j: (i, 0)),
            pl.BlockSpec((gw, tn), lambda i, j: (0, j)),
            pl.BlockSpec((sw, tn), lambda i, j: (1, j)),
            pl.BlockSpec((tm, tn), lambda i, j: (i, j)),
        ],
        out_specs=pl.BlockSpec((tm, tn), lambda i, j: (i, j)),
        out_shape=jax.ShapeDtypeStruct((rows, d), F32),
        compiler_params=_params("parallel", "arbitrary"),
        name="mixer_out_proj",
    )(o_gdn, o_sb, w_out, w_out, h)


def _gdn_chunk_all_heads(xc_ref, ba, valid, z_ref, convw_ref, hp_ref, nw_ref, s_ref, o_ref, *, chunk, heads):
    gw = heads * HEAD_DIM
    width = convw_ref.shape[0]
    base = SUBLANES - (width - 1)
    row = lax.broadcasted_iota(jnp.int32, (chunk, chunk), 0)
    col = lax.broadcasted_iota(jnp.int32, (chunk, chunk), 1)
    causal = row >= col
    strict = row > col
    eye = row == col
    lower_ones = jnp.where(causal, 1.0, 0.0).astype(F32)
    all_ones = jnp.ones((chunk, chunk), F32)
    ident = jnp.where(eye, 1.0, 0.0).astype(F32)

    beta_all = jax.nn.sigmoid(ba)
    g_all = -jnp.exp(hp_ref[0:1, :]) * _softplus(ba + hp_ref[1:2, :]) * valid
    gc_all = jnp.dot(lower_ones, g_all, precision=HIGHEST, preferred_element_type=F32)

    def conv_part(part, h):
        c0 = part * gw + h * HEAD_DIM
        acc = None
        for i in range(width):
            term = xc_ref[base + i:base + i + chunk, c0:c0 + HEAD_DIM] * convw_ref[i:i + 1, c0:c0 + HEAD_DIM]
            acc = term if acc is None else acc + term
        return _silu(acc) * valid

    for h in range(heads):
        q = conv_part(0, h)
        k = conv_part(1, h)
        v = conv_part(2, h)
        q = q * lax.rsqrt(jnp.sum(q * q, axis=-1, keepdims=True) + L2_EPS) * GDN_Q_SCALE
        k = k * lax.rsqrt(jnp.sum(k * k, axis=-1, keepdims=True) + L2_EPS)
        beta = beta_all[:, h:h + 1]
        gc = gc_all[:, heads + h:heads + h + 1]
        g_end = gc[chunk - 1:chunk, :]
        gc_row = jnp.dot(all_ones, jnp.where(eye, gc, 0.0), precision=HIGHEST, preferred_element_type=F32)
        decay = jnp.exp(jnp.where(causal, gc - gc_row, -jnp.inf))
        kb = k.astype(BF16)
        kk = lax.dot_general(kb, kb, (((1,), (1,)), ((), ())), preferred_element_type=F32)
        qk = lax.dot_general(q.astype(BF16), kb, (((1,), (1,)), ((), ())), preferred_element_type=F32)
        n = jnp.where(strict, -(beta * kk * decay), 0.0)
        t = ident + n
        m = n
        for _ in range(int(math.log2(chunk)) - 1):
            m = jnp.dot(m, m, precision=HIGHEST, preferred_element_type=F32)
            t = t + jnp.dot(t, m, precision=HIGHEST, preferred_element_type=F32)
        egc = jnp.exp(gc)
        rhs = jnp.concatenate([beta * v, (beta * egc) * k], axis=1)
        sol = jnp.dot(t, rhs, precision=HIGHEST, preferred_element_type=F32)
        u = sol[:, :HEAD_DIM]
        w = sol[:, HEAD_DIM:]
        s = s_ref[h]
        sb = s.astype(BF16)
        v_new = u - jnp.dot(w.astype(BF16), sb, preferred_element_type=F32)
        vb = v_new.astype(BF16)
        o = (jnp.dot((q * egc).astype(BF16), sb, preferred_element_type=F32)
             + jnp.dot((qk * decay).astype(BF16), vb, preferred_element_type=F32))
        kd = (k * jnp.exp(g_end - gc)).astype(BF16)
        s_ref[h] = jnp.exp(g_end) * s + lax.dot_general(kd, vb, (((0,), (0,)), ((), ())),
                                                        preferred_element_type=F32)
        zg = z_ref[:, h * HEAD_DIM:(h + 1) * HEAD_DIM]
        on = o * lax.rsqrt(jnp.mean(o * o, axis=-1, keepdims=True) + NORM_EPS) * nw_ref[...]
        o_ref[:, h * HEAD_DIM:(h + 1) * HEAD_DIM] = (on * _silu(zg)).astype(o_ref.dtype)


def _gdn_prompt_kernel(x_ref, ba_ref, z_ref, convw_ref, hp_ref, nw_ref, o_ref, s_ref, cs_ref, xc_ref,
                       *, chunk, heads, n_real):
    c = pl.program_id(0)

    @pl.when(c == 0)
    def _():
        s_ref[...] = jnp.zeros_like(s_ref)
        xc_ref[0:SUBLANES, :] = jnp.zeros((SUBLANES, xc_ref.shape[1]), F32)

    xc_ref[SUBLANES:SUBLANES + chunk, :] = x_ref[...]
    pos = lax.broadcasted_iota(jnp.int32, (chunk, 1), 0) + c * chunk
    valid = jnp.where(pos < n_real, 1.0, 0.0).astype(F32)
    _gdn_chunk_all_heads(xc_ref, ba_ref[...], valid, z_ref, convw_ref, hp_ref, nw_ref, s_ref, o_ref,
                         chunk=chunk, heads=heads)

    last_local = (n_real - 1) % chunk + 1

    @pl.when(c == (n_real - 1) // chunk)
    def _():
        cs_ref[...] = xc_ref[last_local:last_local + SUBLANES, :]

    xc_ref[0:SUBLANES, :] = xc_ref[chunk:chunk + SUBLANES, :]


def _gdn_prompt(p, conv_w, hp, norm_w, *, heads, n_real, ba_block):
    rows = p.shape[0]
    gw = heads * HEAD_DIM
    ch = 3 * gw
    chunk = GDN_CHUNK
    kern = functools.partial(_gdn_prompt_kernel, chunk=chunk, heads=heads, n_real=n_real)
    return pl.pallas_call(
        kern,
        grid=(rows // chunk,),
        in_specs=[
            pl.BlockSpec((chunk, ch), lambda c: (c, 0)),
            pl.BlockSpec((chunk, LANES), lambda c: (c, ba_block)),
            pl.BlockSpec((chunk, gw), lambda c: (c, 3)),
            pl.BlockSpec(conv_w.shape, lambda c: (0, 0)),
            pl.BlockSpec((2, LANES), lambda c: (0, 0)),
            pl.BlockSpec((1, HEAD_DIM), lambda c: (0, 0)),
        ],
        out_specs=[
            pl.BlockSpec((chunk, gw), lambda c: (c, 0)),
            pl.BlockSpec((heads, HEAD_DIM, HEAD_DIM), lambda c: (0, 0, 0)),
            pl.BlockSpec((SUBLANES, ch), lambda c: (0, 0)),
        ],
        out_shape=[
            jax.ShapeDtypeStruct((rows, gw), BF16),
            jax.ShapeDtypeStruct((heads, HEAD_DIM, HEAD_DIM), F32),
            jax.ShapeDtypeStruct((SUBLANES, ch), F32),
        ],
        scratch_shapes=[pltpu.VMEM((SUBLANES + chunk, ch), F32)],
        compiler_params=_params("arbitrary"),
        name="gdn_prompt",
    )(p, p, p, conv_w, hp, norm_w.reshape(1, HEAD_DIM))


def _gdn_sample_kernel(x_ref, prev_ref, ba_ref, z_ref, convw_ref, hp_ref, nw_ref, sin_ref,
                       o_ref, s_ref, cs_ref, xc_ref, *, heads, n_tok):
    width = convw_ref.shape[0]
    xc_ref[0:SUBLANES, :] = jnp.zeros((SUBLANES, xc_ref.shape[1]), F32)
    xc_ref[SUBLANES - (width - 1):SUBLANES, :] = prev_ref[...]
    xc_ref[SUBLANES:SUBLANES + SAMPLE_ROWS, :] = x_ref[...]
    s_ref[...] = sin_ref[...]
    pos = lax.broadcasted_iota(jnp.int32, (SAMPLE_ROWS, 1), 0)
    valid = jnp.where(pos < n_tok, 1.0, 0.0).astype(F32)
    _gdn_chunk_all_heads(xc_ref, ba_ref[...], valid, z_ref, convw_ref, hp_ref, nw_ref, s_ref, o_ref,
                         chunk=SAMPLE_ROWS, heads=heads)
    end = SUBLANES + n_tok
    cs_ref[...] = xc_ref[end - (width - 1):end, :]


def _gdn_sample(ps, prev, state, conv_w, hp, norm_w, *, heads, n_tok, ba_block):
    bsz = ps.shape[0]
    gw = heads * HEAD_DIM
    ch = 3 * gw
    width = conv_w.shape[0]
    kern = functools.partial(_gdn_sample_kernel, heads=heads, n_tok=n_tok)
    return pl.pallas_call(
        kern,
        grid=(bsz,),
        in_specs=[
            pl.BlockSpec((None, SAMPLE_ROWS, ch), lambda b: (b, 0, 0)),
            pl.BlockSpec((None, width - 1, ch), lambda b: (b, 0, 0)),
            pl.BlockSpec((None, SAMPLE_ROWS, LANES), lambda b: (b, 0, ba_block)),
            pl.BlockSpec((None, SAMPLE_ROWS, gw), lambda b: (b, 0, 3)),
            pl.BlockSpec(conv_w.shape, lambda b: (0, 0)),
            pl.BlockSpec((2, LANES), lambda b: (0, 0)),
            pl.BlockSpec((1, HEAD_DIM), lambda b: (0, 0)),
            pl.BlockSpec((None, heads, HEAD_DIM, HEAD_DIM), lambda b: (b, 0, 0, 0)),
        ],
        out_specs=[
            pl.BlockSpec((None, SAMPLE_ROWS, gw), lambda b: (b, 0, 0)),
            pl.BlockSpec((None, heads, HEAD_DIM, HEAD_DIM), lambda b: (b, 0, 0, 0)),
            pl.BlockSpec((None, width - 1, ch), lambda b: (b, 0, 0)),
        ],
        out_shape=[
            jax.ShapeDtypeStruct((bsz, SAMPLE_ROWS, gw), F32),
            jax.ShapeDtypeStruct((bsz, heads, HEAD_DIM, HEAD_DIM), F32),
            jax.ShapeDtypeStruct((bsz, width - 1, ch), F32),
        ],
        scratch_shapes=[pltpu.VMEM((SUBLANES + SAMPLE_ROWS, ch), F32)],
        compiler_params=_params("parallel"),
        name="gdn_sample",
    )(ps, prev, ps, ps, conv_w, hp, norm_w.reshape(1, HEAD_DIM), state)


def _sb_block(qb, k, v, bias, tri, carry, acc, mask):
    n = k.shape[0]
    z = lax.dot_general(qb, k, (((1,), (1,)), ((), ())), preferred_element_type=F32) * SB_SCALE + bias
    sp = _softplus(z)
    if mask is not None:
        sp = jnp.where(mask, sp, 0.0)
    hi = sp.astype(BF16)
    lo = (sp - hi.astype(F32)).astype(BF16)
    ce = jnp.dot(hi, tri, preferred_element_type=F32) + jnp.dot(lo, tri, preferred_element_type=F32)
    w = jnp.exp(z - (carry + ce[:, :n]))
    if mask is not None:
        w = jnp.where(mask, w, 0.0)
    acc = acc + jnp.dot(w.astype(BF16), v, preferred_element_type=F32)
    return carry + ce[:, n:], acc


def _sb_prompt_kernel(q_ref, k_ref, v_ref, bias_ref, nw_ref, tri_ref, o_ref, *, block):
    i = pl.program_id(1)
    qb = q_ref[...].astype(BF16)
    bias = bias_ref[...]
    tri = tri_ref[...]

    def visit(kb, carry, acc, mask):
        start = pl.multiple_of(kb * block, block)
        k = k_ref[pl.ds(start, block), :].astype(BF16)
        v = v_ref[pl.ds(start, block), :].astype(BF16)
        return _sb_block(qb, k, v, bias, tri, carry, acc, mask)

    row = lax.broadcasted_iota(jnp.int32, (block, block), 0)
    col = lax.broadcasted_iota(jnp.int32, (block, block), 1)
    carry = jnp.zeros((block, block), F32)
    acc = jnp.zeros((block, HEAD_DIM), F32)
    carry, acc = visit(i, carry, acc, col < row)
    carry, acc = lax.fori_loop(0, i, lambda t, ca: visit(i - 1 - t, ca[0], ca[1], None), (carry, acc))
    o = acc * lax.rsqrt(jnp.mean(acc * acc, axis=-1, keepdims=True) + NORM_EPS) * nw_ref[...]
    o_ref[...] = o.astype(o_ref.dtype)


def _sb_tri(n):
    r = jnp.arange(n)[:, None]
    c = jnp.arange(n)[None, :]
    return jnp.concatenate([(r >= c).astype(BF16), jnp.ones((n, n), BF16)], axis=1)


def _sb_prompt(p, bias_rows, norm_w, *, heads, q_block0, k_block0, v_block0):
    rows = p.shape[0]
    block = SB_BLOCK
    kern = functools.partial(_sb_prompt_kernel, block=block)
    return pl.pallas_call(
        kern,
        grid=(heads, rows // block),
        in_specs=[
            pl.BlockSpec((block, HEAD_DIM), lambda h, i: (i, q_block0 + h)),
            pl.BlockSpec((rows, HEAD_DIM), lambda h, i: (0, k_block0 + h)),
            pl.BlockSpec((rows, HEAD_DIM), lambda h, i: (0, v_block0 + h)),
            pl.BlockSpec((None, 1, block), lambda h, i: (h, 0, 0)),
            pl.BlockSpec((1, HEAD_DIM), lambda h, i: (0, 0)),
            pl.BlockSpec((block, 2 * block), lambda h, i: (0, 0)),
        ],
        out_specs=pl.BlockSpec((block, HEAD_DIM), lambda h, i: (i, h)),
        out_shape=jax.ShapeDtypeStruct((rows, heads * HEAD_DIM), BF16),
        compiler_params=_params("parallel", "arbitrary"),
        name="sb_prompt",
    )(p, p, p, bias_rows, norm_w.reshape(1, HEAD_DIM), _sb_tri(block))


def _sb_sample_kernel(pt_ref, q_ref, kn_ref, vn_ref, kp_ref, vp_ref, bias_ref, nw_ref, tri_ref,
                      bd_ref, rowtok_ref, rowhead_ref, o_ref,
                      qbd_ref, knew_ref, vnew_ref, carry_ref, acc_ref, *, heads, n_tok):
    del pt_ref
    s = pl.program_id(1)
    page = kp_ref.shape[0] // heads
    sw = heads * HEAD_DIM
    bias = bias_ref[...]
    tri = tri_ref[...]

    def visit(k, v, mask):
        carry, acc = _sb_block(qbd_ref[...], k.astype(BF16), v.astype(BF16), bias, tri,
                               carry_ref[...], acc_ref[...], mask)
        carry_ref[...] = carry
        acc_ref[...] = acc

    @pl.when(s == 0)
    def _():
        reps = [jnp.broadcast_to(q_ref[t:t + 1, :], (heads, sw)) for t in range(n_tok)]
        qbd_ref[...] = (jnp.concatenate(reps, axis=0) * bd_ref[...]).astype(BF16)
        carry_ref[...] = jnp.zeros_like(carry_ref)
        acc_ref[...] = jnp.zeros_like(acc_ref)
        knew_ref[...] = jnp.zeros_like(knew_ref)
        vnew_ref[...] = jnp.zeros_like(vnew_ref)
        knew_ref[0:SAMPLE_ROWS, :] = kn_ref[...]
        vnew_ref[0:SAMPLE_ROWS, :] = vn_ref[...]
        col = lax.broadcasted_iota(jnp.int32, (n_tok * heads, page), 1)
        visit(knew_ref[...], vnew_ref[...], col < rowtok_ref[...])

    def load_page(ref):
        return jnp.concatenate([ref[pl.ds(h, page, stride=heads), :] for h in range(heads)], axis=1)

    @pl.when(s > 0)
    def _():
        visit(load_page(kp_ref), load_page(vp_ref), None)

    @pl.when(s == pl.num_programs(1) - 1)
    def _():
        acc = acc_ref[...]
        rowhead = rowhead_ref[...]
        out = jnp.zeros((n_tok * heads, HEAD_DIM), F32)
        for h in range(heads):
            out = out + jnp.where(rowhead == h, acc[:, h * HEAD_DIM:(h + 1) * HEAD_DIM], 0.0)
        o_ref[...] = out * lax.rsqrt(jnp.mean(out * out, axis=-1, keepdims=True) + NORM_EPS) * nw_ref[...]


def _sb_sample(ps, cache_k, cache_v, page_table, sb_bias, norm_w, *, heads, n_tok,
               q_block0, k_block0, v_block0):
    bsz = ps.shape[0]
    n_pages = page_table.shape[1]
    page = cache_k.shape[1] // heads
    sw = heads * HEAD_DIM
    nr = n_tok * heads
    assert page % LANES == 0
    r = jnp.arange(nr)
    bias_rows = jnp.broadcast_to(sb_bias.astype(F32)[r % heads][:, None], (nr, page))
    bd = (r[:, None] % heads == jnp.arange(sw)[None, :] // HEAD_DIM).astype(F32)
    rowtok = jnp.broadcast_to((r // heads).astype(jnp.int32)[:, None], (nr, page))
    rowhead = jnp.broadcast_to((r % heads).astype(jnp.int32)[:, None], (nr, HEAD_DIM))

    def page_idx(b, s, pt):
        return (pt[b, jnp.minimum(n_pages - s, n_pages - 1)], 0, 0)

    kern = functools.partial(_sb_sample_kernel, heads=heads, n_tok=n_tok)
    grid_spec = pltpu.PrefetchScalarGridSpec(
        num_scalar_prefetch=1,
        grid=(bsz, n_pages + 1),
        in_specs=[
            pl.BlockSpec((None, SAMPLE_ROWS, sw), lambda b, s, pt: (b, 0, q_block0)),
            pl.BlockSpec((None, SAMPLE_ROWS, sw), lambda b, s, pt: (b, 0, k_block0)),
            pl.BlockSpec((None, SAMPLE_ROWS, sw), lambda b, s, pt: (b, 0, v_block0)),
            pl.BlockSpec((None, page * heads, HEAD_DIM), page_idx),
            pl.BlockSpec((None, page * heads, HEAD_DIM), page_idx),
            pl.BlockSpec((nr, page), lambda b, s, pt: (0, 0)),
            pl.BlockSpec((1, HEAD_DIM), lambda b, s, pt: (0, 0)),
            pl.BlockSpec((page, 2 * page), lambda b, s, pt: (0, 0)),
            pl.BlockSpec((nr, sw), lambda b, s, pt: (0, 0)),
            pl.BlockSpec((nr, page), lambda b, s, pt: (0, 0)),
            pl.BlockSpec((nr, HEAD_DIM), lambda b, s, pt: (0, 0)),
        ],
        out_specs=pl.BlockSpec((None, nr, HEAD_DIM), lambda b, s, pt: (b, 0, 0)),
        scratch_shapes=[
            pltpu.VMEM((nr, sw), BF16),
            pltpu.VMEM((page, sw), F32),
            pltpu.VMEM((page, sw), F32),
            pltpu.VMEM((nr, page), F32),
            pltpu.VMEM((nr, sw), F32),
        ],
    )
    return pl.pallas_call(
        kern,
        grid_spec=grid_spec,
        out_shape=jax.ShapeDtypeStruct((bsz, nr, HEAD_DIM), F32),
        compiler_params=_params("parallel", "arbitrary"),
        name="sb_sample",
    )(page_table, ps, ps, ps, cache_k, cache_v, bias_rows, norm_w.reshape(1, HEAD_DIM), _sb_tri(page),
      bd, rowtok, rowhead)


def kernel(x_prompt, x_sample, cache_sb_k, cache_sb_v, page_table, state_gdn, state_gdn_conv, meta_tokens, norm_ffn1, ffn1_w_gate, ffn1_w_up, ffn1_w_down, norm_mix, w_in, gdn_conv_w, gdn_a_log, gdn_dt_bias, gdn_norm_w, sb_norm_w, sb_bias, w_out, norm_ffn2, ffn2_w_gate, ffn2_w_up, ffn2_w_down, norm_final):
    bsz, seq, d = x_prompt.shape
    assert bsz == 1, "one prompt sequence is supported"
    dec_b, dec_t, _ = x_sample.shape
    assert dec_t <= SAMPLE_ROWS
    depth = norm_ffn1.shape[0]
    n_meta = meta_tokens.shape[0]
    hg = gdn_a_log.shape[1]
    hs = sb_bias.shape[1]
    gw, sw = hg * HEAD_DIM, hs * HEAD_DIM
    assert gw == sw and 2 * hg <= LANES
    width = gdn_conv_w.shape[1]
    assert width - 1 <= SUBLANES
    ch = 3 * gw
    page = cache_sb_k.shape[2]
    n_real = n_meta + seq
    s0 = _round_up(n_real, LANES)
    n_samp = dec_b * dec_t
    rows = _round_up(s0 + n_samp, SB_BLOCK)
    dt = x_prompt.dtype

    off_z, off_q, off_k, off_v, off_ba = ch, ch + gw, ch + gw + sw, ch + gw + 2 * sw, ch + gw + 3 * sw
    n_proj = _round_up(off_ba + LANES, 768 if (off_ba + LANES) > 768 else LANES)

    h = jnp.concatenate([
        meta_tokens.astype(dt), x_prompt[0], jnp.zeros((s0 - n_real, d), dt),
        x_sample.reshape(n_samp, d), jnp.zeros((rows - s0 - n_samp, d), dt)], axis=0)

    outs = [[] for _ in range(8)]
    for layer in range(depth):
        w = w_in[layer]
        w_r = jnp.concatenate([
            w[:, :ch + gw], w[:, ch + gw + 2 * hg:], w[:, ch + gw:ch + gw + 2 * hg],
            jnp.zeros((d, n_proj - off_ba - 2 * hg), w.dtype)], axis=1).astype(BF16)
        hp = jnp.zeros((2, LANES), F32)
        hp = hp.at[0, hg:2 * hg].set(gdn_a_log[layer].astype(F32)).at[1, hg:2 * hg].set(gdn_dt_bias[layer].astype(F32))
        conv_w = gdn_conv_w[layer].astype(F32)
        last = layer == depth - 1

        h = _ffn(h, norm_ffn1[layer], ffn1_w_gate[layer].astype(BF16), ffn1_w_up[layer].astype(BF16),
                 ffn1_w_down[layer].astype(BF16), norm_final, False)
        p = _proj(h, norm_mix[layer], w_r)

        o_gdn, s_p, c_p = _gdn_prompt(p, conv_w, hp, gdn_norm_w[layer], heads=hg, n_real=n_real,
                                      ba_block=off_ba // LANES)
        bias_rows = jnp.broadcast_to(sb_bias[layer].astype(F32)[:, None, None], (hs, 1, SB_BLOCK))
        o_sb = _sb_prompt(p, bias_rows, sb_norm_w[layer], heads=hs, q_block0=off_q // HEAD_DIM,
                          k_block0=off_k // HEAD_DIM, v_block0=off_v // HEAD_DIM)

        ps = p[s0:s0 + n_samp].reshape(dec_b, dec_t, n_proj)
        ps = jnp.pad(ps, ((0, 0), (0, SAMPLE_ROWS - dec_t), (0, 0)))
        og_s, s_s, c_s = _gdn_sample(ps, state_gdn_conv[layer].astype(F32), state_gdn[layer].astype(F32),
                                     conv_w, hp, gdn_norm_w[layer], heads=hg, n_tok=dec_t,
                                     ba_block=off_ba // LANES)
        n_pool = cache_sb_k.shape[1]
        os_s = _sb_sample(ps, cache_sb_k[layer].reshape(n_pool, page * hs, HEAD_DIM),
                          cache_sb_v[layer].reshape(n_pool, page * hs, HEAD_DIM), page_table, sb_bias[layer],
                          sb_norm_w[layer], heads=hs, n_tok=dec_t, q_block0=off_q // sw,
                          k_block0=off_k // sw, v_block0=off_v // sw)
        o_gdn = lax.dynamic_update_slice(o_gdn, og_s[:, :dec_t].reshape(n_samp, gw).astype(BF16), (s0, 0))
        o_sb = lax.dynamic_update_slice(o_sb, os_s.reshape(n_samp, sw).astype(BF16), (s0, 0))

        h = _outproj(o_gdn, o_sb, w_out[layer].astype(BF16), h)
        h = _ffn(h, norm_ffn2[layer], ffn2_w_gate[layer].astype(BF16), ffn2_w_up[layer].astype(BF16),
                 ffn2_w_down[layer].astype(BF16), norm_final, last)

        outs[0].append(p[:n_real, off_k:off_k + sw].reshape(1, n_real, hs, HEAD_DIM).astype(cache_sb_k.dtype))
        outs[1].append(p[:n_real, off_v:off_v + sw].reshape(1, n_real, hs, HEAD_DIM).astype(cache_sb_v.dtype))
        outs[2].append(p[s0:s0 + n_samp, off_k:off_k + sw].reshape(dec_b, dec_t, hs, HEAD_DIM).astype(cache_sb_k.dtype))
        outs[3].append(p[s0:s0 + n_samp, off_v:off_v + sw].reshape(dec_b, dec_t, hs, HEAD_DIM).astype(cache_sb_v.dtype))
        outs[4].append(s_p[None].astype(state_gdn.dtype))
        outs[5].append(s_s.astype(state_gdn.dtype))
        outs[6].append(c_p[None, SUBLANES - (width - 1):].astype(state_gdn_conv.dtype))
        outs[7].append(c_s.astype(state_gdn_conv.dtype))

    y_prompt = h[n_meta:n_real].reshape(1, seq, d)
    y_sample = h[s0:s0 + n_samp].reshape(dec_b, dec_t, d)
    return (y_prompt, y_sample) + tuple(jnp.stack(o) for o in outs)
```

```python
import functools
import math

import jax
import jax.numpy as jnp
from jax import lax
from jax.experimental import pallas as pl
from jax.experimental.pallas import tpu as pltpu

F32 = jnp.float32
BF16 = jnp.bfloat16
HIGHEST = lax.Precision.HIGHEST

HEAD_DIM = 128
LANES = 128
SUBLANES = 8
NORM_EPS = 1e-6
L2_EPS = 1e-6
FFN_RESIDUAL_WEIGHT = 0.5
GDN_CHUNK = 64
GDN_Q_SCALE = HEAD_DIM ** -0.5
SB_SCALE = HEAD_DIM ** -0.5
SB_BLOCK = 256
SAMPLE_ROWS = 8
VMEM_LIMIT_BYTES = 58 * 1024 * 1024


def _round_up(x, m):
    return (x + m - 1) // m * m


def _row_tile(rows, cap):
    best = 16
    for t in range(16, cap + 1, 16):
        if rows % t == 0:
            best = t
    assert rows % best == 0
    return best


def _col_tile(cols, cap):
    best = LANES
    for t in range(LANES, cap + 1, LANES):
        if cols % t == 0:
            best = t
    assert cols % best == 0
    return best


def _softplus(x):
    return jnp.maximum(x, 0.0) + jnp.log1p(jnp.exp(-jnp.abs(x)))


def _silu(x):
    return x * jax.nn.sigmoid(x)


def _rms_rows(x, w):
    return x * lax.rsqrt(jnp.mean(x * x, axis=-1, keepdims=True) + NORM_EPS) * w


def _params(*semantics):
    return pltpu.CompilerParams(dimension_semantics=semantics, vmem_limit_bytes=VMEM_LIMIT_BYTES)


def _ffn_kernel(x_ref, nw_ref, wg_ref, wu_ref, wd_ref, fw_ref, o_ref, u_ref, *, final_norm):
    j = pl.program_id(1)

    @pl.when(j == 0)
    def _():
        x = x_ref[...]
        u_ref[...] = _rms_rows(x, nw_ref[...]).astype(BF16)
        o_ref[...] = x

    u = u_ref[...]
    g = jnp.dot(u, wg_ref[...], preferred_element_type=F32)
    up = jnp.dot(u, wu_ref[...], preferred_element_type=F32)
    a = (_silu(g) * up).astype(BF16)
    o_ref[...] += FFN_RESIDUAL_WEIGHT * jnp.dot(a, wd_ref[...], preferred_element_type=F32)

    if final_norm:
        @pl.when(j == pl.num_programs(1) - 1)
        def _():
            o_ref[...] = _rms_rows(o_ref[...], fw_ref[...])


def _ffn(h, norm_w, wg, wu, wd, final_w, final_norm):
    rows, d = h.shape
    f = wg.shape[1]
    tm = _row_tile(rows, 640)
    tf = _col_tile(f, 256)
    return pl.pallas_call(
        functools.partial(_ffn_kernel, final_norm=final_norm),
        grid=(rows // tm, f // tf),
        in_specs=[
            pl.BlockSpec((tm, d), lambda i, j: (i, 0), pipeline_mode=pl.Buffered(1)),
            pl.BlockSpec((1, d), lambda i, j: (0, 0)),
            pl.BlockSpec((d, tf), lambda i, j: (0, j)),
            pl.BlockSpec((d, tf), lambda i, j: (0, j)),
            pl.BlockSpec((tf, d), lambda i, j: (j, 0)),
            pl.BlockSpec((1, d), lambda i, j: (0, 0)),
        ],
        out_specs=pl.BlockSpec((tm, d), lambda i, j: (i, 0)),
        out_shape=jax.ShapeDtypeStruct((rows, d), F32),
        scratch_shapes=[pltpu.VMEM((tm, d), BF16)],
        compiler_params=_params("parallel", "arbitrary"),
        name="ffn_half_step",
    )(h, norm_w.reshape(1, d), wg, wu, wd, final_w.reshape(1, d))


def _proj_kernel(x_ref, nw_ref, w_ref, o_ref, u_ref):
    @pl.when(pl.program_id(1) == 0)
    def _():
        u_ref[...] = _rms_rows(x_ref[...], nw_ref[...]).astype(BF16)

    o_ref[...] = jnp.dot(u_ref[...], w_ref[...], preferred_element_type=F32)


def _proj(h, norm_w, w):
    rows, d = h.shape
    n = w.shape[1]
    tm = _row_tile(rows, 640)
    tn = _col_tile(n, 768)
    return pl.pallas_call(
        _proj_kernel,
        grid=(rows // tm, n // tn),
        in_specs=[
            pl.BlockSpec((tm, d), lambda i, j: (i, 0)),
            pl.BlockSpec((1, d), lambda i, j: (0, 0)),
            pl.BlockSpec((d, tn), lambda i, j: (0, j)),
        ],
        out_specs=pl.BlockSpec((tm, tn), lambda i, j: (i, j)),
        out_shape=jax.ShapeDtypeStruct((rows, n), F32),
        scratch_shapes=[pltpu.VMEM((tm, d), BF16)],
        compiler_params=_params("parallel", "arbitrary"),
        name="mixer_in_proj",
    )(h, norm_w.reshape(1, d), w)


def _outproj_kernel(og_ref, os_ref, wg_ref, ws_ref, h_ref, o_ref):
    o_ref[...] = (h_ref[...]
                  + jnp.dot(og_ref[...], wg_ref[...], preferred_element_type=F32)
                  + jnp.dot(os_ref[...], ws_ref[...], preferred_element_type=F32))


def _outproj(o_gdn, o_sb, w_out, h):
    rows, d = h.shape
    gw, sw = o_gdn.shape[1], o_sb.shape[1]
    assert gw == sw, "head groups are assumed equally wide"
    tm = _row_tile(rows, 640)
    tn = _col_tile(d, 1024)
    return pl.pallas_call(
        _outproj_kernel,
        grid=(rows // tm, d // tn),
        in_specs=[
            pl.BlockSpec((tm, gw), lambda i, j: (i, 0)),
            pl.BlockSpec((tm, sw), lambda i, j: (i, 0)),
            pl.BlockSpec((gw, tn), lambda i, j: (0, j)),
            pl.BlockSpec((sw, tn), lambda i, j: (1, j)),
            pl.BlockSpec((tm, tn), lambda i, j: (i, j)),
        ],
        out_specs=pl.BlockSpec((tm, tn), lambda i, j: (i, j)),
        out_shape=jax.ShapeDtypeStruct((rows, d), F32),
        compiler_params=_params("parallel", "arbitrary"),
        name="mixer_out_proj",
    )(o_gdn, o_sb, w_out, w_out, h)


def _split2(x):
    hi = x.astype(BF16)
    return hi, (x - hi.astype(F32)).astype(BF16)


def _split3(x):
    hi = x.astype(BF16)
    r = x - hi.astype(F32)
    mid = r.astype(BF16)
    return hi, mid, (r - mid.astype(F32)).astype(BF16)


def _dot(a, b):
    return jnp.dot(a, b, preferred_element_type=F32)


def _dot_nt(a, b):
    return lax.dot_general(a, b, (((1,), (1,)), ((), ())), preferred_element_type=F32)


def _dot3(a, b):
    a_hi, a_lo = _split2(a)
    b_hi, b_lo = _split2(b)
    return _dot(a_hi, b_hi) + _dot(a_hi, b_lo) + _dot(a_lo, b_hi)


def _dot_ones(ones_bf16, x):
    hi, mid, lo = _split3(x)
    return _dot(ones_bf16, hi) + _dot(ones_bf16, mid) + _dot(ones_bf16, lo)


def _gdn_chunk_all_heads(xc_ref, ba, valid, z_ref, convw_ref, hp_ref, nw_ref, s_ref, o_ref, *, chunk, heads):
    gw = heads * HEAD_DIM
    width = convw_ref.shape[0]
    base = SUBLANES - (width - 1)
    hs = range(heads)
    row = lax.broadcasted_iota(jnp.int32, (chunk, chunk), 0)
    col = lax.broadcasted_iota(jnp.int32, (chunk, chunk), 1)
    strict = row > col
    ident = jnp.where(row == col, 1.0, 0.0).astype(F32)
    lower_ones = jnp.where(row >= col, 1.0, 0.0).astype(BF16)
    row_p = lax.broadcasted_iota(jnp.int32, (chunk, LANES), 0)
    col_p = lax.broadcasted_iota(jnp.int32, (chunk, LANES), 1)
    causal_p = row_p >= col_p
    after_p = jnp.where(row_p > col_p, 1.0, 0.0).astype(F32)

    beta_all = jax.nn.sigmoid(ba)
    g_all = -jnp.exp(hp_ref[0:1, :]) * _softplus(ba + hp_ref[1:2, :]) * valid
    gc_all = _dot_ones(lower_ones, g_all)
    beta = [beta_all[:, h:h + 1] for h in hs]
    gc = [gc_all[:, heads + h:heads + h + 1] for h in hs]
    g_end = [gc[h][chunk - 1:chunk, :] for h in hs]
    gu = jnp.concatenate([g_all[:, heads + h:heads + h + 1] * after_p for h in hs], axis=1)
    diff_all = _dot_ones(lower_ones, gu)
    decay = [jnp.exp(jnp.where(causal_p, diff_all[:, h * LANES:(h + 1) * LANES], -jnp.inf))[:, :chunk]
             for h in hs]

    def conv_part(part, h):
        c0 = part * gw + h * HEAD_DIM
        acc = None
        for i in range(width):
            term = xc_ref[base + i:base + i + chunk, c0:c0 + HEAD_DIM] * convw_ref[i:i + 1, c0:c0 + HEAD_DIM]
            acc = term if acc is None else acc + term
        return _silu(acc) * valid

    q = [conv_part(0, h) for h in hs]
    k = [conv_part(1, h) for h in hs]
    v = [conv_part(2, h) for h in hs]
    q = [x * lax.rsqrt(jnp.sum(x * x, axis=-1, keepdims=True) + L2_EPS) * GDN_Q_SCALE for x in q]
    k = [x * lax.rsqrt(jnp.sum(x * x, axis=-1, keepdims=True) + L2_EPS) for x in k]
    kb = [x.astype(BF16) for x in k]
    kk = [_dot_nt(kb[h], kb[h]) for h in hs]
    qk = [_dot_nt(q[h].astype(BF16), kb[h]) for h in hs]

    n_fac = int(math.log2(chunk))
    m = [jnp.where(strict, -(beta[h] * kk[h] * decay[h]), 0.0) for h in hs]
    t = [ident + m[h] for h in hs]
    if n_fac > 1:
        m = [_dot3(m[h], m[h]) for h in hs]
    for f in range(1, n_fac):
        if f < n_fac - 1:
            prod = [_dot3(jnp.concatenate([t[h], m[h]], axis=0), m[h]) for h in hs]
            t = [t[h] + prod[h][:chunk] for h in hs]
            m = [prod[h][chunk:] for h in hs]
        else:
            t = [t[h] + _dot3(t[h], m[h]) for h in hs]

    egc = [jnp.exp(gc[h]) for h in hs]
    rhs = [jnp.concatenate([beta[h] * v[h], (beta[h] * egc[h]) * k[h]], axis=1) for h in hs]
    sol = [_dot3(t[h], rhs[h]) for h in hs]
    s = [s_ref[h] for h in hs]
    sb = [x.astype(BF16) for x in s]
    v_new = [sol[h][:, :HEAD_DIM] - _dot(sol[h][:, HEAD_DIM:].astype(BF16), sb[h]) for h in hs]
    vb = [x.astype(BF16) for x in v_new]
    o = [_dot((q[h] * egc[h]).astype(BF16), sb[h]) + _dot((qk[h] * decay[h]).astype(BF16), vb[h]) for h in hs]
    kd = [(k[h] * jnp.exp(g_end[h] - gc[h])).astype(BF16) for h in hs]
    for h in hs:
        s_ref[h] = jnp.exp(g_end[h]) * s[h] + lax.dot_general(kd[h], vb[h], (((0,), (0,)), ((), ())),
                                                              preferred_element_type=F32)
    for h in hs:
        zg = z_ref[:, h * HEAD_DIM:(h + 1) * HEAD_DIM]
        on = o[h] * lax.rsqrt(jnp.mean(o[h] * o[h], axis=-1, keepdims=True) + NORM_EPS) * nw_ref[...]
        o_ref[:, h * HEAD_DIM:(h + 1) * HEAD_DIM] = (on * _silu(zg)).astype(o_ref.dtype)


def _gdn_prompt_kernel(x_ref, ba_ref, z_ref, convw_ref, hp_ref, nw_ref, o_ref, s_ref, cs_ref, xc_ref,
                       *, chunk, heads, n_real):
    c = pl.program_id(0)

    @pl.when(c == 0)
    def _():
        s_ref[...] = jnp.zeros_like(s_ref)
        xc_ref[0:SUBLANES, :] = jnp.zeros((SUBLANES, xc_ref.shape[1]), F32)

    xc_ref[SUBLANES:SUBLANES + chunk, :] = x_ref[...]
    pos = lax.broadcasted_iota(jnp.int32, (chunk, 1), 0) + c * chunk
    valid = jnp.where(pos < n_real, 1.0, 0.0).astype(F32)
    _gdn_chunk_all_heads(xc_ref, ba_ref[...], valid, z_ref, convw_ref, hp_ref, nw_ref, s_ref, o_ref,
                         chunk=chunk, heads=heads)

    last_local = (n_real - 1) % chunk + 1

    @pl.when(c == (n_real - 1) // chunk)
    def _():
        cs_ref[...] = xc_ref[last_local:last_local + SUBLANES, :]

    xc_ref[0:SUBLANES, :] = xc_ref[chunk:chunk + SUBLANES, :]


def _gdn_prompt(p, conv_w, hp, norm_w, *, heads, n_real, ba_block):
    rows = p.shape[0]
    gw = heads * HEAD_DIM
    ch = 3 * gw
    chunk = GDN_CHUNK
    kern = functools.partial(_gdn_prompt_kernel, chunk=chunk, heads=heads, n_real=n_real)
    return pl.pallas_call(
        kern,
        grid=(rows // chunk,),
        in_specs=[
            pl.BlockSpec((chunk, ch), lambda c: (c, 0)),
            pl.BlockSpec((chunk, LANES), lambda c: (c, ba_block)),
            pl.BlockSpec((chunk, gw), lambda c: (c, 3)),
            pl.BlockSpec(conv_w.shape, lambda c: (0, 0)),
            pl.BlockSpec((2, LANES), lambda c: (0, 0)),
            pl.BlockSpec((1, HEAD_DIM), lambda c: (0, 0)),
        ],
        out_specs=[
            pl.BlockSpec((chunk, gw), lambda c: (c, 0)),
            pl.BlockSpec((heads, HEAD_DIM, HEAD_DIM), lambda c: (0, 0, 0)),
            pl.BlockSpec((SUBLANES, ch), lambda c: (0, 0)),
        ],
        out_shape=[
            jax.ShapeDtypeStruct((rows, gw), BF16),
            jax.ShapeDtypeStruct((heads, HEAD_DIM, HEAD_DIM), F32),
            jax.ShapeDtypeStruct((SUBLANES, ch), F32),
        ],
        scratch_shapes=[pltpu.VMEM((SUBLANES + chunk, ch), F32)],
        compiler_params=_params("arbitrary"),
        name="gdn_prompt",
    )(p, p, p, conv_w, hp, norm_w.reshape(1, HEAD_DIM))


def _gdn_sample_kernel(x_ref, prev_ref, ba_ref, z_ref, convw_ref, hp_ref, nw_ref, sin_ref,
                       o_ref, s_ref, cs_ref, xc_ref, *, heads, n_tok):
    width = convw_ref.shape[0]
    xc_ref[0:SUBLANES, :] = jnp.zeros((SUBLANES, xc_ref.shape[1]), F32)
    xc_ref[SUBLANES - (width - 1):SUBLANES, :] = prev_ref[...]
    xc_ref[SUBLANES:SUBLANES + SAMPLE_ROWS, :] = x_ref[...]
    s_ref[...] = sin_ref[...]
    pos = lax.broadcasted_iota(jnp.int32, (SAMPLE_ROWS, 1), 0)
    valid = jnp.where(pos < n_tok, 1.0, 0.0).astype(F32)
    _gdn_chunk_all_heads(xc_ref, ba_ref[...], valid, z_ref, convw_ref, hp_ref, nw_ref, s_ref, o_ref,
                         chunk=SAMPLE_ROWS, heads=heads)
    end = SUBLANES + n_tok
    cs_ref[...] = xc_ref[end - (width - 1):end, :]


def _gdn_sample(ps, prev, state, conv_w, hp, norm_w, *, heads, n_tok, ba_block):
    bsz = ps.shape[0]
    gw = heads * HEAD_DIM
    ch = 3 * gw
    width = conv_w.shape[0]
    kern = functools.partial(_gdn_sample_kernel, heads=heads, n_tok=n_tok)
    return pl.pallas_call(
        kern,
        grid=(bsz,),
        in_specs=[
            pl.BlockSpec((None, SAMPLE_ROWS, ch), lambda b: (b, 0, 0)),
            pl.BlockSpec((None, width - 1, ch), lambda b: (b, 0, 0)),
            pl.BlockSpec((None, SAMPLE_ROWS, LANES), lambda b: (b, 0, ba_block)),
            pl.BlockSpec((None, SAMPLE_ROWS, gw), lambda b: (b, 0, 3)),
            pl.BlockSpec(conv_w.shape, lambda b: (0, 0)),
            pl.BlockSpec((2, LANES), lambda b: (0, 0)),
            pl.BlockSpec((1, HEAD_DIM), lambda b: (0, 0)),
            pl.BlockSpec((None, heads, HEAD_DIM, HEAD_DIM), lambda b: (b, 0, 0, 0)),
        ],
        out_specs=[
            pl.BlockSpec((None, SAMPLE_ROWS, gw), lambda b: (b, 0, 0)),
            pl.BlockSpec((None, heads, HEAD_DIM, HEAD_DIM), lambda b: (b, 0, 0, 0)),
            pl.BlockSpec((None, width - 1, ch), lambda b: (b, 0, 0)),
        ],
        out_shape=[
            jax.ShapeDtypeStruct((bsz, SAMPLE_ROWS, gw), F32),
            jax.ShapeDtypeStruct((bsz, heads, HEAD_DIM, HEAD_DIM), F32),
            jax.ShapeDtypeStruct((bsz, width - 1, ch), F32),
        ],
        scratch_shapes=[pltpu.VMEM((SUBLANES + SAMPLE_ROWS, ch), F32)],
        compiler_params=_params("parallel"),
        name="gdn_sample",
    )(ps, prev, ps, ps, conv_w, hp, norm_w.reshape(1, HEAD_DIM), state)


def _softplus_sb(x):
    return jnp.maximum(x, 0.0) + jnp.log(1.0 + jnp.exp(-jnp.abs(x)))


def _sb_block(qb, k, v, bias, tri, carry, acc, mask):
    n = k.shape[0]
    z = _dot_nt(qb, k) * SB_SCALE + bias
    sp = _softplus_sb(z)
    if mask is not None:
        sp = jnp.where(mask, sp, 0.0)
    hi, lo = _split2(sp)
    cl = _dot(hi, tri) + _dot(lo, tri)
    w = jnp.exp(z - (jnp.concatenate([carry] * (n // LANES), axis=1) + cl))
    if mask is not None:
        w = jnp.where(mask, w, 0.0)
    acc = acc + _dot(w.astype(BF16), v)
    return carry + jnp.broadcast_to(cl[:, :1], carry.shape), acc


def _sb_logits(qb, k, bias, tri, mask):
    z = _dot_nt(qb, k) * SB_SCALE + bias
    if mask is not None:
        z = jnp.where(mask, z, -jnp.inf)
    hi, lo = _split2(_softplus_sb(z))
    cl = _dot(hi, tri) + _dot(lo, tri)
    return z - cl, jnp.broadcast_to(cl[:, :1], (z.shape[0], LANES))


def _sb_accumulate(zc, tot, v, carry, acc):
    n = zc.shape[1]
    w = jnp.exp(zc - jnp.concatenate([carry] * (n // LANES), axis=1))
    return carry + tot, acc + _dot(w.astype(BF16), v)


def _sb_prompt_kernel(q_ref, k_ref, v_ref, bias_ref, nw_ref, tri_ref, o_ref, kb_ref, vb_ref, *, block, group):
    i = pl.program_id(1)

    @pl.when(i == 0)
    def _():
        kb_ref[...] = k_ref[...].astype(BF16)
        vb_ref[...] = v_ref[...].astype(BF16)

    tri = tri_ref[...]
    gs = range(group)
    lanes = [slice(g * HEAD_DIM, (g + 1) * HEAD_DIM) for g in gs]
    qb = [q_ref[:, lanes[g]].astype(BF16) for g in gs]
    bias = [bias_ref[g] for g in gs]

    def logits(kb, mask):
        start = pl.multiple_of(kb * block, block)
        return tuple(_sb_logits(qb[g], kb_ref[pl.ds(start, block), lanes[g]], bias[g], tri, mask) for g in gs)

    def accumulate(kb, pend, state):
        start = pl.multiple_of(kb * block, block)
        return tuple(_sb_accumulate(pend[g][0], pend[g][1], vb_ref[pl.ds(start, block), lanes[g]],
                                    state[g][0], state[g][1]) for g in gs)

    row = lax.broadcasted_iota(jnp.int32, (block, block), 0)
    col = lax.broadcasted_iota(jnp.int32, (block, block), 1)
    pend = logits(i, col < row)
    state = tuple((jnp.zeros((block, LANES), F32), jnp.zeros((block, HEAD_DIM), F32)) for _ in gs)

    def body(t, ps):
        pend, state = ps
        return logits(i - 1 - t, None), accumulate(i - t, pend, state)

    pend, state = lax.fori_loop(0, i, body, (pend, state))
    state = accumulate(0, pend, state)
    for g in gs:
        acc = state[g][1]
        o = acc * lax.rsqrt(jnp.mean(acc * acc, axis=-1, keepdims=True) + NORM_EPS) * nw_ref[...]
        o_ref[:, lanes[g]] = o.astype(o_ref.dtype)


def _sb_tri(n):
    r = jnp.arange(n)[:, None]
    c = jnp.arange(n)[None, :]
    return (r >= c).astype(BF16)


def _sb_prompt(p, bias_rows, norm_w, *, heads, q_off, k_off, v_off):
    rows = p.shape[0]
    block = SB_BLOCK
    group = 2 if heads % 2 == 0 else 1
    gl = group * HEAD_DIM
    assert q_off % gl == 0 and k_off % gl == 0 and v_off % gl == 0
    kern = functools.partial(_sb_prompt_kernel, block=block, group=group)
    return pl.pallas_call(
        kern,
        grid=(heads // group, rows // block),
        in_specs=[
            pl.BlockSpec((block, gl), lambda h, i: (i, q_off // gl + h)),
            pl.BlockSpec((rows, gl), lambda h, i: (0, k_off // gl + h), pipeline_mode=pl.Buffered(1)),
            pl.BlockSpec((rows, gl), lambda h, i: (0, v_off // gl + h), pipeline_mode=pl.Buffered(1)),
            pl.BlockSpec((group, 1, block), lambda h, i: (h, 0, 0)),
            pl.BlockSpec((1, HEAD_DIM), lambda h, i: (0, 0)),
            pl.BlockSpec((block, block), lambda h, i: (0, 0)),
        ],
        out_specs=pl.BlockSpec((block, gl), lambda h, i: (i, h)),
        out_shape=jax.ShapeDtypeStruct((rows, heads * HEAD_DIM), BF16),
        scratch_shapes=[pltpu.VMEM((rows, gl), BF16), pltpu.VMEM((rows, gl), BF16)],
        compiler_params=_params("parallel", "arbitrary"),
        name="sb_prompt",
    )(p, p, p, bias_rows, norm_w.reshape(1, HEAD_DIM), _sb_tri(block))


def _sb_sample_kernel(pt_ref, q_ref, kn_ref, vn_ref, *rest, heads, n_tok, n_slot):
    del pt_ref
    kp_refs, vp_refs = rest[:n_slot], rest[n_slot:2 * n_slot]
    (bias_ref, nw_ref, tri_ref, bd_ref, rowtok_ref, rowhead_ref, o_ref,
     qbd_ref, knew_ref, vnew_ref, carry_ref, acc_ref) = rest[2 * n_slot:]
    s = pl.program_id(1)
    page = kp_refs[0].shape[0] // heads
    sw = heads * HEAD_DIM

    def visit(k, v, bias, tri, mask):
        carry, acc = _sb_block(qbd_ref[...], k.astype(BF16), v.astype(BF16), bias, tri,
                               carry_ref[...], acc_ref[...], mask)
        carry_ref[...] = carry
        acc_ref[...] = acc

    @pl.when(s == 0)
    def _():
        reps = [jnp.broadcast_to(q_ref[t:t + 1, :], (heads, sw)) for t in range(n_tok)]
        qbd_ref[...] = (jnp.concatenate(reps, axis=0) * bd_ref[...]).astype(BF16)
        carry_ref[...] = jnp.zeros_like(carry_ref)
        acc_ref[...] = jnp.zeros_like(acc_ref)
        knew_ref[...] = jnp.zeros_like(knew_ref)
        vnew_ref[...] = jnp.zeros_like(vnew_ref)
        knew_ref[0:SAMPLE_ROWS, :] = kn_ref[...]
        vnew_ref[0:SAMPLE_ROWS, :] = vn_ref[...]
        col = lax.broadcasted_iota(jnp.int32, (n_tok * heads, page), 1)
        visit(knew_ref[...], vnew_ref[...], bias_ref[:, 0:page], tri_ref[0:page, 0:page],
              col < rowtok_ref[...])

    def load_pages(refs):
        return jnp.concatenate(
            [jnp.concatenate([r[pl.ds(h, page, stride=heads), :] for h in range(heads)], axis=1) for r in refs],
            axis=0)

    @pl.when(s > 0)
    def _():
        visit(load_pages(kp_refs), load_pages(vp_refs), bias_ref[...], tri_ref[...], None)

    @pl.when(s == pl.num_programs(1) - 1)
    def _():
        acc = acc_ref[...]
        rowhead = rowhead_ref[...]
        out = jnp.zeros((n_tok * heads, HEAD_DIM), F32)
        for h in range(heads):
            out = out + jnp.where(rowhead == h, acc[:, h * HEAD_DIM:(h + 1) * HEAD_DIM], 0.0)
        o_ref[...] = out * lax.rsqrt(jnp.mean(out * out, axis=-1, keepdims=True) + NORM_EPS) * nw_ref[...]


def _sb_sample(ps, cache_k, cache_v, page_table, sb_bias, norm_w, *, heads, n_tok,
               q_block0, k_block0, v_block0):
    bsz = ps.shape[0]
    n_pages = page_table.shape[1]
    page = cache_k.shape[1] // heads
    sw = heads * HEAD_DIM
    nr = n_tok * heads
    assert page % LANES == 0
    n_slot = 4 if n_pages % 4 == 0 else (2 if n_pages % 2 == 0 else 1)
    n_step = n_pages // n_slot
    keys = n_slot * page
    r = jnp.arange(nr)
    bias_rows = jnp.broadcast_to(sb_bias.astype(F32)[r % heads][:, None], (nr, keys))
    bd = (r[:, None] % heads == jnp.arange(sw)[None, :] // HEAD_DIM).astype(F32)
    rowtok = jnp.broadcast_to((r // heads).astype(jnp.int32)[:, None], (nr, page))
    rowhead = jnp.broadcast_to((r % heads).astype(jnp.int32)[:, None], (nr, HEAD_DIM))

    def page_idx(slot):
        return lambda b, s, pt: (pt[b, n_pages - jnp.maximum(s, 1) * n_slot + slot], 0, 0)

    kern = functools.partial(_sb_sample_kernel, heads=heads, n_tok=n_tok, n_slot=n_slot)
    const = lambda b, s, pt: (0, 0)
    page_specs = [pl.BlockSpec((None, page * heads, HEAD_DIM), page_idx(j)) for j in range(n_slot)]
    grid_spec = pltpu.PrefetchScalarGridSpec(
        num_scalar_prefetch=1,
        grid=(bsz, n_step + 1),
        in_specs=[
            pl.BlockSpec((None, SAMPLE_ROWS, sw), lambda b, s, pt: (b, 0, q_block0)),
            pl.BlockSpec((None, SAMPLE_ROWS, sw), lambda b, s, pt: (b, 0, k_block0)),
            pl.BlockSpec((None, SAMPLE_ROWS, sw), lambda b, s, pt: (b, 0, v_block0)),
            *page_specs, *page_specs,
            pl.BlockSpec((nr, keys), const),
            pl.BlockSpec((1, HEAD_DIM), const),
            pl.BlockSpec((keys, keys), const),
            pl.BlockSpec((nr, sw), const),
            pl.BlockSpec((nr, page), const),
            pl.BlockSpec((nr, HEAD_DIM), const),
        ],
        out_specs=pl.BlockSpec((None, nr, HEAD_DIM), lambda b, s, pt: (b, 0, 0)),
        scratch_shapes=[
            pltpu.VMEM((nr, sw), BF16),
            pltpu.VMEM((page, sw), F32),
            pltpu.VMEM((page, sw), F32),
            pltpu.VMEM((nr, LANES), F32),
            pltpu.VMEM((nr, sw), F32),
        ],
    )
    return pl.pallas_call(
        kern,
        grid_spec=grid_spec,
        out_shape=jax.ShapeDtypeStruct((bsz, nr, HEAD_DIM), F32),
        compiler_params=_params("parallel", "arbitrary"),
        name="sb_sample",
    )(page_table, ps, ps, ps, *([cache_k] * n_slot), *([cache_v] * n_slot), bias_rows,
      norm_w.reshape(1, HEAD_DIM), _sb_tri(keys), bd, rowtok, rowhead)


def kernel(x_prompt, x_sample, cache_sb_k, cache_sb_v, page_table, state_gdn, state_gdn_conv, meta_tokens, norm_ffn1, ffn1_w_gate, ffn1_w_up, ffn1_w_down, norm_mix, w_in, gdn_conv_w, gdn_a_log, gdn_dt_bias, gdn_norm_w, sb_norm_w, sb_bias, w_out, norm_ffn2, ffn2_w_gate, ffn2_w_up, ffn2_w_down, norm_final):
    bsz, seq, d = x_prompt.shape
    assert bsz == 1, "one prompt sequence is supported"
    dec_b, dec_t, _ = x_sample.shape
    assert dec_t <= SAMPLE_ROWS
    depth = norm_ffn1.shape[0]
    n_meta = meta_tokens.shape[0]
    hg = gdn_a_log.shape[1]
    hs = sb_bias.shape[1]
    gw, sw = hg * HEAD_DIM, hs * HEAD_DIM
    assert gw == sw and 2 * hg <= LANES
    width = gdn_conv_w.shape[1]
    assert width - 1 <= SUBLANES
    ch = 3 * gw
    page = cache_sb_k.shape[2]
    n_real = n_meta + seq
    s0 = _round_up(n_real, LANES)
    n_samp = dec_b * dec_t
    rows = _round_up(s0 + n_samp, SB_BLOCK)
    dt = x_prompt.dtype

    off_z, off_q, off_k, off_v, off_ba = ch, ch + gw, ch + gw + sw, ch + gw + 2 * sw, ch + gw + 3 * sw
    n_proj = _round_up(off_ba + LANES, 768 if (off_ba + LANES) > 768 else LANES)

    h = jnp.concatenate([
        meta_tokens.astype(dt), x_prompt[0], jnp.zeros((s0 - n_real, d), dt),
        x_sample.reshape(n_samp, d), jnp.zeros((rows - s0 - n_samp, d), dt)], axis=0)

    outs = [[] for _ in range(8)]
    for layer in range(depth):
        w = w_in[layer]
        w_r = jnp.concatenate([
            w[:, :ch + gw], w[:, ch + gw + 2 * hg:], w[:, ch + gw:ch + gw + 2 * hg],
            jnp.zeros((d, n_proj - off_ba - 2 * hg), w.dtype)], axis=1).astype(BF16)
        hp = jnp.zeros((2, LANES), F32)
        hp = hp.at[0, hg:2 * hg].set(gdn_a_log[layer].astype(F32)).at[1, hg:2 * hg].set(gdn_dt_bias[layer].astype(F32))
        conv_w = gdn_conv_w[layer].astype(F32)
        last = layer == depth - 1

        h = _ffn(h, norm_ffn1[layer], ffn1_w_gate[layer].astype(BF16), ffn1_w_up[layer].astype(BF16),
                 ffn1_w_down[layer].astype(BF16), norm_final, False)
        p = _proj(h, norm_mix[layer], w_r)

        o_gdn, s_p, c_p = _gdn_prompt(p, conv_w, hp, gdn_norm_w[layer], heads=hg, n_real=n_real,
                                      ba_block=off_ba // LANES)
        bias_rows = jnp.broadcast_to(sb_bias[layer].astype(F32)[:, None, None], (hs, 1, SB_BLOCK))
        o_sb = _sb_prompt(p, bias_rows, sb_norm_w[layer], heads=hs, q_off=off_q, k_off=off_k, v_off=off_v)

        ps = p[s0:s0 + n_samp].reshape(dec_b, dec_t, n_proj)
        ps = jnp.pad(ps, ((0, 0), (0, SAMPLE_ROWS - dec_t), (0, 0)))
        og_s, s_s, c_s = _gdn_sample(ps, state_gdn_conv[layer].astype(F32), state_gdn[layer].astype(F32),
                                     conv_w, hp, gdn_norm_w[layer], heads=hg, n_tok=dec_t,
                                     ba_block=off_ba // LANES)
        n_pool = cache_sb_k.shape[1]
        os_s = _sb_sample(ps, cache_sb_k[layer].reshape(n_pool, page * hs, HEAD_DIM),
                          cache_sb_v[layer].reshape(n_pool, page * hs, HEAD_DIM), page_table, sb_bias[layer],
                          sb_norm_w[layer], heads=hs, n_tok=dec_t, q_block0=off_q // sw,
                          k_block0=off_k // sw, v_block0=off_v // sw)
        o_gdn = lax.dynamic_update_slice(o_gdn, og_s[:, :dec_t].reshape(n_samp, gw).astype(BF16), (s0, 0))
        o_sb = lax.dynamic_update_slice(o_sb, os_s.reshape(n_samp, sw).astype(BF16), (s0, 0))

        h = _outproj(o_gdn, o_sb, w_out[layer].astype(BF16), h)
        h = _ffn(h, norm_ffn2[layer], ffn2_w_gate[layer].astype(BF16), ffn2_w_up[layer].astype(BF16),
                 ffn2_w_down[layer].astype(BF16), norm_final, last)

        outs[0].append(p[:n_real, off_k:off_k + sw].reshape(1, n_real, hs, HEAD_DIM).astype(cache_sb_k.dtype))
        outs[1].append(p[:n_real, off_v:off_v + sw].reshape(1, n_real, hs, HEAD_DIM).astype(cache_sb_v.dtype))
        outs[2].append(p[s0:s0 + n_samp, off_k:off_k + sw].reshape(dec_b, dec_t, hs, HEAD_DIM).astype(cache_sb_k.dtype))
        outs[3].append(p[s0:s0 + n_samp, off_v:off_v + sw].reshape(dec_b, dec_t, hs, HEAD_DIM).astype(cache_sb_v.dtype))
        outs[4].append(s_p[None].astype(state_gdn.dtype))
        outs[5].append(s_s.astype(state_gdn.dtype))
        outs[6].append(c_p[None, SUBLANES - (width - 1):].astype(state_gdn_conv.dtype))
        outs[7].append(c_s.astype(state_gdn_conv.dtype))

    y_prompt = h[n_meta:n_real].reshape(1, seq, d)
    y_sample = h[s0:s0 + n_samp].reshape(dec_b, dec_t, d)
    return (y_prompt, y_sample) + tuple(jnp.stack(o) for o in outs)
```

```python
import functools
import math

import jax
import jax.numpy as jnp
from jax import lax
from jax.experimental import pallas as pl
from jax.experimental.pallas import tpu as pltpu

F32 = jnp.float32
BF16 = jnp.bfloat16
HIGHEST = lax.Precision.HIGHEST

HEAD_DIM = 128
LANES = 128
SUBLANES = 8
NORM_EPS = 1e-6
L2_EPS = 1e-6
FFN_RESIDUAL_WEIGHT = 0.5
GDN_CHUNK = 64
GDN_Q_SCALE = HEAD_DIM ** -0.5
SB_SCALE = HEAD_DIM ** -0.5
SB_BLOCK = 256
SAMPLE_ROWS = 8
VMEM_LIMIT_BYTES = 58 * 1024 * 1024


def _round_up(x, m):
    return (x + m - 1) // m * m


def _row_tile(rows, cap):
    best = 16
    for t in range(16, cap + 1, 16):
        if rows % t == 0:
            best = t
    assert rows % best == 0
    return best


def _col_tile(cols, cap):
    best = LANES
    for t in range(LANES, cap + 1, LANES):
        if cols % t == 0:
            best = t
    assert cols % best == 0
    return best


def _softplus(x):
    return jnp.maximum(x, 0.0) + jnp.log1p(jnp.exp(-jnp.abs(x)))


def _silu(x):
    return x * jax.nn.sigmoid(x)


def _rms_rows(x, w):
    return x * lax.rsqrt(jnp.mean(x * x, axis=-1, keepdims=True) + NORM_EPS) * w


def _params(*semantics):
    return pltpu.CompilerParams(dimension_semantics=semantics, vmem_limit_bytes=VMEM_LIMIT_BYTES)


def _ffn_kernel(x_ref, nw_ref, wg_ref, wu_ref, wd_ref, fw_ref, o_ref, u_ref, *, final_norm):
    j = pl.program_id(1)

    @pl.when(j == 0)
    def _():
        x = x_ref[...]
        u_ref[...] = _rms_rows(x, nw_ref[...]).astype(BF16)
        o_ref[...] = x

    u = u_ref[...]
    g = jnp.dot(u, wg_ref[...], preferred_element_type=F32)
    up = jnp.dot(u, wu_ref[...], preferred_element_type=F32)
    a = (_silu(g) * up).astype(BF16)
    o_ref[...] += FFN_RESIDUAL_WEIGHT * jnp.dot(a, wd_ref[...], preferred_element_type=F32)

    if final_norm:
        @pl.when(j == pl.num_programs(1) - 1)
        def _():
            o_ref[...] = _rms_rows(o_ref[...], fw_ref[...])


def _ffn(h, norm_w, wg, wu, wd, final_w, final_norm):
    rows, d = h.shape
    f = wg.shape[1]
    tm = _row_tile(rows, 640)
    tf = _col_tile(f, 256)
    return pl.pallas_call(
        functools.partial(_ffn_kernel, final_norm=final_norm),
        grid=(rows // tm, f // tf),
        in_specs=[
            pl.BlockSpec((tm, d), lambda i, j: (i, 0), pipeline_mode=pl.Buffered(1)),
            pl.BlockSpec((1, d), lambda i, j: (0, 0)),
            pl.BlockSpec((d, tf), lambda i, j: (0, j)),
            pl.BlockSpec((d, tf), lambda i, j: (0, j)),
            pl.BlockSpec((tf, d), lambda i, j: (j, 0)),
            pl.BlockSpec((1, d), lambda i, j: (0, 0)),
        ],
        out_specs=pl.BlockSpec((tm, d), lambda i, j: (i, 0)),
        out_shape=jax.ShapeDtypeStruct((rows, d), F32),
        scratch_shapes=[pltpu.VMEM((tm, d), BF16)],
        compiler_params=_params("parallel", "arbitrary"),
        name="ffn_half_step",
    )(h, norm_w.reshape(1, d), wg, wu, wd, final_w.reshape(1, d))


def _proj_kernel(x_ref, nw_ref, w_ref, o_ref, ob_ref, u_ref):
    @pl.when(pl.program_id(1) == 0)
    def _():
        u_ref[...] = _rms_rows(x_ref[...], nw_ref[...]).astype(BF16)

    p = jnp.dot(u_ref[...], w_ref[...], preferred_element_type=F32)
    o_ref[...] = p
    ob_ref[...] = p.astype(BF16)


def _proj(h, norm_w, w):
    rows, d = h.shape
    n = w.shape[1]
    tm = _row_tile(rows, 640)
    tn = _col_tile(n, 768)
    return pl.pallas_call(
        _proj_kernel,
        grid=(rows // tm, n // tn),
        in_specs=[
            pl.BlockSpec((tm, d), lambda i, j: (i, 0)),
            pl.BlockSpec((1, d), lambda i, j: (0, 0)),
            pl.BlockSpec((d, tn), lambda i, j: (0, j)),
        ],
        out_specs=[pl.BlockSpec((tm, tn), lambda i, j: (i, j)), pl.BlockSpec((tm, tn), lambda i, j: (i, j))],
        out_shape=[jax.ShapeDtypeStruct((rows, n), F32), jax.ShapeDtypeStruct((rows, n), BF16)],
        scratch_shapes=[pltpu.VMEM((tm, d), BF16)],
        compiler_params=_params("parallel", "arbitrary"),
        name="mixer_in_proj",
    )(h, norm_w.reshape(1, d), w)


def _outproj_kernel(og_ref, os_ref, wg_ref, ws_ref, h_ref, o_ref):
    o_ref[...] = (h_ref[...]
                  + jnp.dot(og_ref[...], wg_ref[...], preferred_element_type=F32)
                  + jnp.dot(os_ref[...], ws_ref[...], preferred_element_type=F32))


def _outproj(o_gdn, o_sb, w_out, h):
    rows, d = h.shape
    gw, sw = o_gdn.shape[1], o_sb.shape[1]
    assert gw == sw, "head groups are assumed equally wide"
    tm = _row_tile(rows, 640)
    tn = _col_tile(d, 1024)
    return pl.pallas_call(
        _outproj_kernel,
        grid=(rows // tm, d // tn),
        in_specs=[
            pl.BlockSpec((tm, gw), lambda i, j: (i, 0)),
            pl.BlockSpec((tm, sw), lambda i, j: (i, 0)),
            pl.BlockSpec((gw, tn), lambda i, j: (0, j)),
            pl.BlockSpec((sw, tn), lambda i, j: (1, j)),
            pl.BlockSpec((tm, tn), lambda i, j: (i, j)),
        ],
        out_specs=pl.BlockSpec((tm, tn), lambda i, j: (i, j)),
        out_shape=jax.ShapeDtypeStruct((rows, d), F32),
        compiler_params=_params("parallel", "arbitrary"),
        name="mixer_out_proj",
    )(o_gdn, o_sb, w_out, w_out, h)


def _split2(x):
    hi = x.astype(BF16)
    return hi, (x - hi.astype(F32)).astype(BF16)


def _split3(x):
    hi = x.astype(BF16)
    r = x - hi.astype(F32)
    mid = r.astype(BF16)
    return hi, mid, (r - mid.astype(F32)).astype(BF16)


def _dot(a, b):
    return jnp.dot(a, b, preferred_element_type=F32)


def _dot_nt(a, b):
    return lax.dot_general(a, b, (((1,), (1,)), ((), ())), preferred_element_type=F32)


def _dot3(a, b):
    a_hi, a_lo = _split2(a)
    b_hi, b_lo = _split2(b)
    return _dot(a_hi, b_hi) + _dot(a_hi, b_lo) + _dot(a_lo, b_hi)


def _dot_ones(ones_bf16, x):
    hi, mid, lo = _split3(x)
    return _dot(ones_bf16, hi) + _dot(ones_bf16, mid) + _dot(ones_bf16, lo)


def _gdn_chunk_all_heads(xc_ref, ba, valid, z_ref, convw_ref, hp_ref, nw_ref, s_ref, o_ref, *, chunk, heads):
    gw = heads * HEAD_DIM
    width = convw_ref.shape[0]
    base = SUBLANES - (width - 1)
    hs = range(heads)
    row = lax.broadcasted_iota(jnp.int32, (chunk, chunk), 0)
    col = lax.broadcasted_iota(jnp.int32, (chunk, chunk), 1)
    strict = row > col
    ident = jnp.where(row == col, 1.0, 0.0).astype(F32)
    lower_ones = jnp.where(row >= col, 1.0, 0.0).astype(BF16)
    row_p = lax.broadcasted_iota(jnp.int32, (chunk, LANES), 0)
    col_p = lax.broadcasted_iota(jnp.int32, (chunk, LANES), 1)
    causal_p = row_p >= col_p
    after_p = jnp.where(row_p > col_p, 1.0, 0.0).astype(F32)

    beta_all = jax.nn.sigmoid(ba)
    g_all = -jnp.exp(hp_ref[0:1, :]) * _softplus(ba + hp_ref[1:2, :]) * valid
    gc_all = _dot_ones(lower_ones, g_all)
    beta = [beta_all[:, h:h + 1] for h in hs]
    gc = [gc_all[:, heads + h:heads + h + 1] for h in hs]
    g_end = [gc[h][chunk - 1:chunk, :] for h in hs]
    gu = jnp.concatenate([g_all[:, heads + h:heads + h + 1] * after_p for h in hs], axis=1)
    diff_all = _dot_ones(lower_ones, gu)
    decay = [jnp.exp(jnp.where(causal_p, diff_all[:, h * LANES:(h + 1) * LANES], -jnp.inf))[:, :chunk]
             for h in hs]

    def conv_part(part, h):
        c0 = part * gw + h * HEAD_DIM
        acc = None
        for i in range(width):
            term = xc_ref[base + i:base + i + chunk, c0:c0 + HEAD_DIM] * convw_ref[i:i + 1, c0:c0 + HEAD_DIM]
            acc = term if acc is None else acc + term
        return _silu(acc) * valid

    q = [conv_part(0, h) for h in hs]
    k = [conv_part(1, h) for h in hs]
    v = [conv_part(2, h) for h in hs]
    q = [x * lax.rsqrt(jnp.sum(x * x, axis=-1, keepdims=True) + L2_EPS) * GDN_Q_SCALE for x in q]
    k = [x * lax.rsqrt(jnp.sum(x * x, axis=-1, keepdims=True) + L2_EPS) for x in k]
    kb = [x.astype(BF16) for x in k]
    kk = [_dot_nt(kb[h], kb[h]) for h in hs]
    qk = [_dot_nt(q[h].astype(BF16), kb[h]) for h in hs]

    n_fac = int(math.log2(chunk))
    m = [jnp.where(strict, -(beta[h] * kk[h] * decay[h]), 0.0) for h in hs]
    t = [ident + m[h] for h in hs]
    if n_fac > 1:
        m = [_dot3(m[h], m[h]) for h in hs]
    for f in range(1, n_fac):
        if f < n_fac - 1:
            prod = [_dot3(jnp.concatenate([t[h], m[h]], axis=0), m[h]) for h in hs]
            t = [t[h] + prod[h][:chunk] for h in hs]
            m = [prod[h][chunk:] for h in hs]
        else:
            t = [t[h] + _dot3(t[h], m[h]) for h in hs]

    egc = [jnp.exp(gc[h]) for h in hs]
    rhs = [jnp.concatenate([beta[h] * v[h], (beta[h] * egc[h]) * k[h]], axis=1) for h in hs]
    sol = [_dot3(t[h], rhs[h]) for h in hs]
    s = [s_ref[h] for h in hs]
    sb = [x.astype(BF16) for x in s]
    v_new = [sol[h][:, :HEAD_DIM] - _dot(sol[h][:, HEAD_DIM:].astype(BF16), sb[h]) for h in hs]
    vb = [x.astype(BF16) for x in v_new]
    o = [_dot((q[h] * egc[h]).astype(BF16), sb[h]) + _dot((qk[h] * decay[h]).astype(BF16), vb[h]) for h in hs]
    kd = [(k[h] * jnp.exp(g_end[h] - gc[h])).astype(BF16) for h in hs]
    for h in hs:
        s_ref[h] = jnp.exp(g_end[h]) * s[h] + lax.dot_general(kd[h], vb[h], (((0,), (0,)), ((), ())),
                                                              preferred_element_type=F32)
    for h in hs:
        zg = z_ref[:, h * HEAD_DIM:(h + 1) * HEAD_DIM]
        on = o[h] * lax.rsqrt(jnp.mean(o[h] * o[h], axis=-1, keepdims=True) + NORM_EPS) * nw_ref[...]
        o_ref[:, h * HEAD_DIM:(h + 1) * HEAD_DIM] = (on * _silu(zg)).astype(o_ref.dtype)


def _gdn_prompt_kernel(x_ref, ba_ref, z_ref, convw_ref, hp_ref, nw_ref, o_ref, s_ref, cs_ref, xc_ref,
                       *, chunk, heads, n_real):
    c = pl.program_id(0)

    @pl.when(c == 0)
    def _():
        s_ref[...] = jnp.zeros_like(s_ref)
        xc_ref[0:SUBLANES, :] = jnp.zeros((SUBLANES, xc_ref.shape[1]), F32)

    xc_ref[SUBLANES:SUBLANES + chunk, :] = x_ref[...]
    pos = lax.broadcasted_iota(jnp.int32, (chunk, 1), 0) + c * chunk
    valid = jnp.where(pos < n_real, 1.0, 0.0).astype(F32)
    _gdn_chunk_all_heads(xc_ref, ba_ref[...], valid, z_ref, convw_ref, hp_ref, nw_ref, s_ref, o_ref,
                         chunk=chunk, heads=heads)

    last_local = (n_real - 1) % chunk + 1

    @pl.when(c == (n_real - 1) // chunk)
    def _():
        cs_ref[...] = xc_ref[last_local:last_local + SUBLANES, :]

    xc_ref[0:SUBLANES, :] = xc_ref[chunk:chunk + SUBLANES, :]


def _gdn_prompt(p, conv_w, hp, norm_w, *, heads, n_real, ba_block):
    rows = p.shape[0]
    gw = heads * HEAD_DIM
    ch = 3 * gw
    chunk = GDN_CHUNK
    kern = functools.partial(_gdn_prompt_kernel, chunk=chunk, heads=heads, n_real=n_real)
    return pl.pallas_call(
        kern,
        grid=(rows // chunk,),
        in_specs=[
            pl.BlockSpec((chunk, ch), lambda c: (c, 0)),
            pl.BlockSpec((chunk, LANES), lambda c: (c, ba_block)),
            pl.BlockSpec((chunk, gw), lambda c: (c, 3)),
            pl.BlockSpec(conv_w.shape, lambda c: (0, 0)),
            pl.BlockSpec((2, LANES), lambda c: (0, 0)),
            pl.BlockSpec((1, HEAD_DIM), lambda c: (0, 0)),
        ],
        out_specs=[
            pl.BlockSpec((chunk, gw), lambda c: (c, 0)),
            pl.BlockSpec((heads, HEAD_DIM, HEAD_DIM), lambda c: (0, 0, 0)),
            pl.BlockSpec((SUBLANES, ch), lambda c: (0, 0)),
        ],
        out_shape=[
            jax.ShapeDtypeStruct((rows, gw), BF16),
            jax.ShapeDtypeStruct((heads, HEAD_DIM, HEAD_DIM), F32),
            jax.ShapeDtypeStruct((SUBLANES, ch), F32),
        ],
        scratch_shapes=[pltpu.VMEM((SUBLANES + chunk, ch), F32)],
        compiler_params=_params("arbitrary"),
        name="gdn_prompt",
    )(p, p, p, conv_w, hp, norm_w.reshape(1, HEAD_DIM))


def _gdn_sample_kernel(x_ref, prev_ref, ba_ref, z_ref, convw_ref, hp_ref, nw_ref, sin_ref,
                       o_ref, s_ref, cs_ref, xc_ref, *, heads, n_tok):
    width = convw_ref.shape[0]
    xc_ref[0:SUBLANES, :] = jnp.zeros((SUBLANES, xc_ref.shape[1]), F32)
    xc_ref[SUBLANES - (width - 1):SUBLANES, :] = prev_ref[...]
    xc_ref[SUBLANES:SUBLANES + SAMPLE_ROWS, :] = x_ref[...]
    s_ref[...] = sin_ref[...]
    pos = lax.broadcasted_iota(jnp.int32, (SAMPLE_ROWS, 1), 0)
    valid = jnp.where(pos < n_tok, 1.0, 0.0).astype(F32)
    _gdn_chunk_all_heads(xc_ref, ba_ref[...], valid, z_ref, convw_ref, hp_ref, nw_ref, s_ref, o_ref,
                         chunk=SAMPLE_ROWS, heads=heads)
    end = SUBLANES + n_tok
    cs_ref[...] = xc_ref[end - (width - 1):end, :]


def _gdn_sample(ps, prev, state, conv_w, hp, norm_w, *, heads, n_tok, ba_block):
    bsz = ps.shape[0]
    gw = heads * HEAD_DIM
    ch = 3 * gw
    width = conv_w.shape[0]
    kern = functools.partial(_gdn_sample_kernel, heads=heads, n_tok=n_tok)
    return pl.pallas_call(
        kern,
        grid=(bsz,),
        in_specs=[
            pl.BlockSpec((None, SAMPLE_ROWS, ch), lambda b: (b, 0, 0)),
            pl.BlockSpec((None, width - 1, ch), lambda b: (b, 0, 0)),
            pl.BlockSpec((None, SAMPLE_ROWS, LANES), lambda b: (b, 0, ba_block)),
            pl.BlockSpec((None, SAMPLE_ROWS, gw), lambda b: (b, 0, 3)),
            pl.BlockSpec(conv_w.shape, lambda b: (0, 0)),
            pl.BlockSpec((2, LANES), lambda b: (0, 0)),
            pl.BlockSpec((1, HEAD_DIM), lambda b: (0, 0)),
            pl.BlockSpec((None, heads, HEAD_DIM, HEAD_DIM), lambda b: (b, 0, 0, 0)),
        ],
        out_specs=[
            pl.BlockSpec((None, SAMPLE_ROWS, gw), lambda b: (b, 0, 0)),
            pl.BlockSpec((None, heads, HEAD_DIM, HEAD_DIM), lambda b: (b, 0, 0, 0)),
            pl.BlockSpec((None, width - 1, ch), lambda b: (b, 0, 0)),
        ],
        out_shape=[
            jax.ShapeDtypeStruct((bsz, SAMPLE_ROWS, gw), F32),
            jax.ShapeDtypeStruct((bsz, heads, HEAD_DIM, HEAD_DIM), F32),
            jax.ShapeDtypeStruct((bsz, width - 1, ch), F32),
        ],
        scratch_shapes=[pltpu.VMEM((SUBLANES + SAMPLE_ROWS, ch), F32)],
        compiler_params=_params("parallel"),
        name="gdn_sample",
    )(ps, prev, ps, ps, conv_w, hp, norm_w.reshape(1, HEAD_DIM), state)


def _softplus_sb(x):
    return jnp.maximum(x, 0.0) + jnp.log(1.0 + jnp.exp(-jnp.abs(x)))


def _sb_block(qb, k, v, bias, tri, carry, acc, mask):
    n = k.shape[0]
    z = _dot_nt(qb, k) * SB_SCALE + bias
    sp = _softplus_sb(z)
    if mask is not None:
        sp = jnp.where(mask, sp, 0.0)
    hi, lo = _split2(sp)
    cl = _dot(hi, tri) + _dot(lo, tri)
    w = jnp.exp(z - (jnp.concatenate([carry] * (n // LANES), axis=1) + cl))
    if mask is not None:
        w = jnp.where(mask, w, 0.0)
    acc = acc + _dot(w.astype(BF16), v)
    return carry + jnp.broadcast_to(cl[:, :1], carry.shape), acc


def _sb_logits(qb, k, bias, tri, carry, mask):
    n = k.shape[0]
    z = _dot_nt(qb, k) * SB_SCALE + bias
    if mask is not None:
        z = jnp.where(mask, z, -jnp.inf)
    cl = _dot(_softplus_sb(z).astype(BF16), tri)
    e = z - (jnp.concatenate([carry] * (n // LANES), axis=1) + cl)
    return e, carry + jnp.broadcast_to(cl[:, :1], carry.shape)


def _sb_accumulate(e, v, acc):
    return acc + _dot(jnp.exp(e).astype(BF16), v)


def _sb_prompt_kernel(q_ref, k_ref, v_ref, bias_ref, nw_ref, tri_ref, o_ref, *, block, group):
    i = pl.program_id(1)
    gs = range(group)
    lanes = [slice(g * HEAD_DIM, (g + 1) * HEAD_DIM) for g in gs]

    def logits(kb, carry, mask):
        start = pl.multiple_of(kb * block, block)
        out = [_sb_logits(q_ref[:, lanes[g]], k_ref[pl.ds(start, block), lanes[g]], bias_ref[g],
                          tri_ref[...], carry[g], mask) for g in gs]
        return tuple(o[0] for o in out), tuple(o[1] for o in out)

    def accumulate(kb, e, acc):
        start = pl.multiple_of(kb * block, block)
        return tuple(_sb_accumulate(e[g], v_ref[pl.ds(start, block), lanes[g]], acc[g]) for g in gs)

    row = lax.broadcasted_iota(jnp.int32, (block, block), 0)
    col = lax.broadcasted_iota(jnp.int32, (block, block), 1)
    e, carry = logits(i, tuple(jnp.zeros((block, LANES), F32) for _ in gs), col < row)
    acc = tuple(jnp.zeros((block, HEAD_DIM), F32) for _ in gs)

    def body(t, st):
        e, carry, acc = st
        e_next, carry = logits(i - 1 - t, carry, None)
        return e_next, carry, accumulate(i - t, e, acc)

    e, carry, acc = lax.fori_loop(0, i, body, (e, carry, acc))
    acc = accumulate(0, e, acc)
    for g in gs:
        o = acc[g] * lax.rsqrt(jnp.mean(acc[g] * acc[g], axis=-1, keepdims=True) + NORM_EPS) * nw_ref[...]
        o_ref[:, lanes[g]] = o.astype(o_ref.dtype)


def _sb_tri(n):
    r = jnp.arange(n)[:, None]
    c = jnp.arange(n)[None, :]
    return (r >= c).astype(BF16)


def _sb_prompt(p, bias_rows, norm_w, *, heads, q_off, k_off, v_off):
    rows = p.shape[0]
    block = SB_BLOCK
    group = 4 if heads % 4 == 0 else (2 if heads % 2 == 0 else 1)
    gl = group * HEAD_DIM
    assert q_off % gl == 0 and k_off % gl == 0 and v_off % gl == 0
    kern = functools.partial(_sb_prompt_kernel, block=block, group=group)
    return pl.pallas_call(
        kern,
        grid=(heads // group, rows // block),
        in_specs=[
            pl.BlockSpec((block, gl), lambda h, i: (i, q_off // gl + h)),
            pl.BlockSpec((rows, gl), lambda h, i: (0, k_off // gl + h)),
            pl.BlockSpec((rows, gl), lambda h, i: (0, v_off // gl + h)),
            pl.BlockSpec((group, 1, block), lambda h, i: (h, 0, 0)),
            pl.BlockSpec((1, HEAD_DIM), lambda h, i: (0, 0)),
            pl.BlockSpec((block, block), lambda h, i: (0, 0)),
        ],
        out_specs=pl.BlockSpec((block, gl), lambda h, i: (i, h)),
        out_shape=jax.ShapeDtypeStruct((rows, heads * HEAD_DIM), BF16),
        compiler_params=_params("parallel", "arbitrary"),
        name="sb_prompt",
    )(p, p, p, bias_rows, norm_w.reshape(1, HEAD_DIM), _sb_tri(block))


def _sb_sample_kernel(pt_ref, q_ref, kn_ref, vn_ref, *rest, heads, n_tok, n_slot):
    del pt_ref
    kp_refs, vp_refs = rest[:n_slot], rest[n_slot:2 * n_slot]
    (bias_ref, nw_ref, tri_ref, bd_ref, rowtok_ref, rowhead_ref, o_ref,
     qbd_ref, knew_ref, vnew_ref, carry_ref, acc_ref) = rest[2 * n_slot:]
    s = pl.program_id(1)
    page = kp_refs[0].shape[0] // heads
    sw = heads * HEAD_DIM

    def visit(k, v, bias, tri, mask):
        carry, acc = _sb_block(qbd_ref[...], k.astype(BF16), v.astype(BF16), bias, tri,
                               carry_ref[...], acc_ref[...], mask)
        carry_ref[...] = carry
        acc_ref[...] = acc

    @pl.when(s == 0)
    def _():
        reps = [jnp.broadcast_to(q_ref[t:t + 1, :], (heads, sw)) for t in range(n_tok)]
        qbd_ref[...] = (jnp.concatenate(reps, axis=0) * bd_ref[...]).astype(BF16)
        carry_ref[...] = jnp.zeros_like(carry_ref)
        acc_ref[...] = jnp.zeros_like(acc_ref)
        knew_ref[...] = jnp.zeros_like(knew_ref)
        vnew_ref[...] = jnp.zeros_like(vnew_ref)
        knew_ref[0:SAMPLE_ROWS, :] = kn_ref[...]
        vnew_ref[0:SAMPLE_ROWS, :] = vn_ref[...]
        col = lax.broadcasted_iota(jnp.int32, (n_tok * heads, page), 1)
        visit(knew_ref[...], vnew_ref[...], bias_ref[:, 0:page], tri_ref[0:page, 0:page],
              col < rowtok_ref[...])

    def load_pages(refs):
        return jnp.concatenate(
            [jnp.concatenate([r[pl.ds(h, page, stride=heads), :] for h in range(heads)], axis=1) for r in refs],
            axis=0)

    @pl.when(s > 0)
    def _():
        visit(load_pages(kp_refs), load_pages(vp_refs), bias_ref[...], tri_ref[...], None)

    @pl.when(s == pl.num_programs(1) - 1)
    def _():
        acc = acc_ref[...]
        rowhead = rowhead_ref[...]
        out = jnp.zeros((n_tok * heads, HEAD_DIM), F32)
        for h in range(heads):
            out = out + jnp.where(rowhead == h, acc[:, h * HEAD_DIM:(h + 1) * HEAD_DIM], 0.0)
        o_ref[...] = out * lax.rsqrt(jnp.mean(out * out, axis=-1, keepdims=True) + NORM_EPS) * nw_ref[...]


def _sb_sample(ps, cache_k, cache_v, page_table, sb_bias, norm_w, *, heads, n_tok,
               q_block0, k_block0, v_block0):
    bsz = ps.shape[0]
    n_pages = page_table.shape[1]
    page = cache_k.shape[1] // heads
    sw = heads * HEAD_DIM
    nr = n_tok * heads
    assert page % LANES == 0
    n_slot = 4 if n_pages % 4 == 0 else (2 if n_pages % 2 == 0 else 1)
    n_step = n_pages // n_slot
    keys = n_slot * page
    r = jnp.arange(nr)
    bias_rows = jnp.broadcast_to(sb_bias.astype(F32)[r % heads][:, None], (nr, keys))
    bd = (r[:, None] % heads == jnp.arange(sw)[None, :] // HEAD_DIM).astype(F32)
    rowtok = jnp.broadcast_to((r // heads).astype(jnp.int32)[:, None], (nr, page))
    rowhead = jnp.broadcast_to((r % heads).astype(jnp.int32)[:, None], (nr, HEAD_DIM))

    def page_idx(slot):
        return lambda b, s, pt: (pt[b, n_pages - jnp.maximum(s, 1) * n_slot + slot], 0, 0)

    kern = functools.partial(_sb_sample_kernel, heads=heads, n_tok=n_tok, n_slot=n_slot)
    const = lambda b, s, pt: (0, 0)
    page_specs = [pl.BlockSpec((None, page * heads, HEAD_DIM), page_idx(j)) for j in range(n_slot)]
    grid_spec = pltpu.PrefetchScalarGridSpec(
        num_scalar_prefetch=1,
        grid=(bsz, n_step + 1),
        in_specs=[
            pl.BlockSpec((None, SAMPLE_ROWS, sw), lambda b, s, pt: (b, 0, q_block0)),
            pl.BlockSpec((None, SAMPLE_ROWS, sw), lambda b, s, pt: (b, 0, k_block0)),
            pl.BlockSpec((None, SAMPLE_ROWS, sw), lambda b, s, pt: (b, 0, v_block0)),
            *page_specs, *page_specs,
            pl.BlockSpec((nr, keys), const),
            pl.BlockSpec((1, HEAD_DIM), const),
            pl.BlockSpec((keys, keys), const),
            pl.BlockSpec((nr, sw), const),
            pl.BlockSpec((nr, page), const),
            pl.BlockSpec((nr, HEAD_DIM), const),
        ],
        out_specs=pl.BlockSpec((None, nr, HEAD_DIM), lambda b, s, pt: (b, 0, 0)),
        scratch_shapes=[
            pltpu.VMEM((nr, sw), BF16),
            pltpu.VMEM((page, sw), F32),
            pltpu.VMEM((page, sw), F32),
            pltpu.VMEM((nr, LANES), F32),
            pltpu.VMEM((nr, sw), F32),
        ],
    )
    return pl.pallas_call(
        kern,
        grid_spec=grid_spec,
        out_shape=jax.ShapeDtypeStruct((bsz, nr, HEAD_DIM), F32),
        compiler_params=_params("parallel", "arbitrary"),
        name="sb_sample",
    )(page_table, ps, ps, ps, *([cache_k] * n_slot), *([cache_v] * n_slot), bias_rows,
      norm_w.reshape(1, HEAD_DIM), _sb_tri(keys), bd, rowtok, rowhead)


def kernel(x_prompt, x_sample, cache_sb_k, cache_sb_v, page_table, state_gdn, state_gdn_conv, meta_tokens, norm_ffn1, ffn1_w_gate, ffn1_w_up, ffn1_w_down, norm_mix, w_in, gdn_conv_w, gdn_a_log, gdn_dt_bias, gdn_norm_w, sb_norm_w, sb_bias, w_out, norm_ffn2, ffn2_w_gate, ffn2_w_up, ffn2_w_down, norm_final):
    bsz, seq, d = x_prompt.shape
    assert bsz == 1, "one prompt sequence is supported"
    dec_b, dec_t, _ = x_sample.shape
    assert dec_t <= SAMPLE_ROWS
    depth = norm_ffn1.shape[0]
    n_meta = meta_tokens.shape[0]
    hg = gdn_a_log.shape[1]
    hs = sb_bias.shape[1]
    gw, sw = hg * HEAD_DIM, hs * HEAD_DIM
    assert gw == sw and 2 * hg <= LANES
    width = gdn_conv_w.shape[1]
    assert width - 1 <= SUBLANES
    ch = 3 * gw
    page = cache_sb_k.shape[2]
    n_real = n_meta + seq
    s0 = _round_up(n_real, LANES)
    n_samp = dec_b * dec_t
    rows = _round_up(s0 + n_samp, SB_BLOCK)
    dt = x_prompt.dtype

    off_z, off_q, off_k, off_v, off_ba = ch, ch + gw, ch + gw + sw, ch + gw + 2 * sw, ch + gw + 3 * sw
    n_proj = _round_up(off_ba + LANES, 768 if (off_ba + LANES) > 768 else LANES)

    h = jnp.concatenate([
        meta_tokens.astype(dt), x_prompt[0], jnp.zeros((s0 - n_real, d), dt),
        x_sample.reshape(n_samp, d), jnp.zeros((rows - s0 - n_samp, d), dt)], axis=0)

    outs = [[] for _ in range(8)]
    for layer in range(depth):
        w = w_in[layer]
        w_r = jnp.concatenate([
            w[:, :ch + gw], w[:, ch + gw + 2 * hg:], w[:, ch + gw:ch + gw + 2 * hg],
            jnp.zeros((d, n_proj - off_ba - 2 * hg), w.dtype)], axis=1).astype(BF16)
        hp = jnp.zeros((2, LANES), F32)
        hp = hp.at[0, hg:2 * hg].set(gdn_a_log[layer].astype(F32)).at[1, hg:2 * hg].set(gdn_dt_bias[layer].astype(F32))
        conv_w = gdn_conv_w[layer].astype(F32)
        last = layer == depth - 1

        h = _ffn(h, norm_ffn1[layer], ffn1_w_gate[layer].astype(BF16), ffn1_w_up[layer].astype(BF16),
                 ffn1_w_down[layer].astype(BF16), norm_final, False)
        p, p16 = _proj(h, norm_mix[layer], w_r)

        o_gdn, s_p, c_p = _gdn_prompt(p, conv_w, hp, gdn_norm_w[layer], heads=hg, n_real=n_real,
                                      ba_block=off_ba // LANES)
        bias_rows = jnp.broadcast_to(sb_bias[layer].astype(F32)[:, None, None], (hs, 1, SB_BLOCK))
        o_sb = _sb_prompt(p16, bias_rows, sb_norm_w[layer], heads=hs, q_off=off_q, k_off=off_k, v_off=off_v)

        ps = p[s0:s0 + n_samp].reshape(dec_b, dec_t, n_proj)
        ps = jnp.pad(ps, ((0, 0), (0, SAMPLE_ROWS - dec_t), (0, 0)))
        og_s, s_s, c_s = _gdn_sample(ps, state_gdn_conv[layer].astype(F32), state_gdn[layer].astype(F32),
                                     conv_w, hp, gdn_norm_w[layer], heads=hg, n_tok=dec_t,
                                     ba_block=off_ba // LANES)
        n_pool = cache_sb_k.shape[1]
        os_s = _sb_sample(ps, cache_sb_k[layer].reshape(n_pool, page * hs, HEAD_DIM),
                          cache_sb_v[layer].reshape(n_pool, page * hs, HEAD_DIM), page_table, sb_bias[layer],
                          sb_norm_w[layer], heads=hs, n_tok=dec_t, q_block0=off_q // sw,
                          k_block0=off_k // sw, v_block0=off_v // sw)
        o_gdn = lax.dynamic_update_slice(o_gdn, og_s[:, :dec_t].reshape(n_samp, gw).astype(BF16), (s0, 0))
        o_sb = lax.dynamic_update_slice(o_sb, os_s.reshape(n_samp, sw).astype(BF16), (s0, 0))

        h = _outproj(o_gdn, o_sb, w_out[layer].astype(BF16), h)
        h = _ffn(h, norm_ffn2[layer], ffn2_w_gate[layer].astype(BF16), ffn2_w_up[layer].astype(BF16),
                 ffn2_w_down[layer].astype(BF16), norm_final, last)

        outs[0].append(p[:n_real, off_k:off_k + sw].reshape(1, n_real, hs, HEAD_DIM).astype(cache_sb_k.dtype))
        outs[1].append(p[:n_real, off_v:off_v + sw].reshape(1, n_real, hs, HEAD_DIM).astype(cache_sb_v.dtype))
        outs[2].append(p[s0:s0 + n_samp, off_k:off_k + sw].reshape(dec_b, dec_t, hs, HEAD_DIM).astype(cache_sb_k.dtype))
        outs[3].append(p[s0:s0 + n_samp, off_v:off_v + sw].reshape(dec_b, dec_t, hs, HEAD_DIM).astype(cache_sb_v.dtype))
        outs[4].append(s_p[None].astype(state_gdn.dtype))
        outs[5].append(s_s.astype(state_gdn.dtype))
        outs[6].append(c_p[None, SUBLANES - (width - 1):].astype(state_gdn_conv.dtype))
        outs[7].append(c_s.astype(state_gdn_conv.dtype))

    y_prompt = h[n_meta:n_real].reshape(1, seq, d)
    y_sample = h[s0:s0 + n_samp].reshape(dec_b, dec_t, d)
    return (y_prompt, y_sample) + tuple(jnp.stack(o) for o in outs)
```
